```python
import math
import jax
import jax.numpy as jnp
from jax import lax
import numpy as np

D_MODEL = 2048
BATCH = 4
SEQ = 2048
DEPTH = 2
DEC_BATCH = 128
DEC_SEQ = 4
PAST_LEN = 16384
PAGE_SIZE = 128

N_MIXERS = 2
N_S5_LAYERS = (DEPTH + N_MIXERS - 1) // N_MIXERS
N_LRU_LAYERS = DEPTH // N_MIXERS
S5_GROUP = 16
S5_GROUPS = D_MODEL // S5_GROUP
S5_STATE = 64
S5_DT_MIN = 1e-3
S5_DT_MAX = 1e-1
D_RNN = D_MODEL
LRU_HEADS = 8
LRU_BLOCK = D_RNN // LRU_HEADS
CONV_WIDTH = 4
LRU_C = 8.0
LRU_MIN_RAD = 0.9
LRU_MAX_RAD = 0.999
PEER_HEADS = 8
N_KEYS = 128
N_EXPERTS = N_KEYS * N_KEYS
D_KEY = 256
D_HALF = D_KEY // 2
TOPK = 16
PEER_TOKEN_BLOCK = 128
RMS_EPS = 1e-6

kernel_name = "s5_rglru_peer_hybrid_step"


def rmsnorm(x, g):
    xf = x.astype(jnp.float32)
    y = xf * lax.rsqrt(jnp.mean(xf * xf, axis=-1, keepdims=True) + RMS_EPS)
    return (y * g.astype(jnp.float32)).astype(x.dtype)


def cmul(ar, ai, br, bi):
    return ar * br - ai * bi, ar * bi + ai * br


def s5_discretize(lam_re, lam_im, log_dt, b_re, b_im):
    dt = jnp.exp(log_dt.astype(jnp.float32))[:, None]
    lr = lam_re.astype(jnp.float32)
    li = lam_im.astype(jnp.float32)
    mag = jnp.exp(lr * dt)
    ab_re = mag * jnp.cos(li * dt)
    ab_im = mag * jnp.sin(li * dt)
    nr, ni = ab_re - 1.0, ab_im
    den = lr * lr + li * li
    f_re = (nr * lr + ni * li) / den
    f_im = (ni * lr - nr * li) / den
    br = b_re.astype(jnp.float32)
    bi = b_im.astype(jnp.float32)
    bb_re = f_re[..., None] * br - f_im[..., None] * bi
    bb_im = f_re[..., None] * bi + f_im[..., None] * br
    return ab_re, ab_im, bb_re, bb_im


def _s5_combine(left, right):
    a1r, a1i, b1r, b1i = left
    a2r, a2i, b2r, b2i = right
    ar, ai = cmul(a2r, a2i, a1r, a1i)
    br, bi = cmul(a2r, a2i, b1r, b1i)
    return ar, ai, br + b2r, bi + b2i


def s5_mixer(h, h0_re, h0_im, lam_re, lam_im, log_dt, b_re, b_im, c_re, c_im, d_skip, w_glu, b_glu):
    bsz, t = h.shape[0], h.shape[1]
    hf = h.astype(jnp.float32)
    u = hf.reshape(bsz, t, S5_GROUPS, S5_GROUP)
    ab_re, ab_im, bb_re, bb_im = s5_discretize(lam_re, lam_im, log_dt, b_re, b_im)
    bu_re = jnp.einsum("btgc,gpc->btgp", u, bb_re)
    bu_im = jnp.einsum("btgc,gpc->btgp", u, bb_im)
    if h0_re is not None:
        c0_re, c0_im = cmul(ab_re, ab_im, h0_re.astype(jnp.float32), h0_im.astype(jnp.float32))
        bu_re = bu_re.at[:, 0].add(c0_re)
        bu_im = bu_im.at[:, 0].add(c0_im)
    a_re = jnp.broadcast_to(ab_re, (1, t) + ab_re.shape)
    a_im = jnp.broadcast_to(ab_im, (1, t) + ab_im.shape)
    _, _, s_re, s_im = lax.associative_scan(_s5_combine, (a_re, a_im, bu_re, bu_im), axis=1)
    y = (jnp.einsum("btgp,gcp->btgc", s_re, c_re.astype(jnp.float32))
         - jnp.einsum("btgp,gcp->btgc", s_im, c_im.astype(jnp.float32)))
    y = y.reshape(bsz, t, D_MODEL) + d_skip.astype(jnp.float32) * hf
    y = jax.nn.gelu(y, approximate=False)
    out = y * jax.nn.sigmoid(y @ w_glu.astype(jnp.float32) + b_glu.astype(jnp.float32))
    return out.astype(h.dtype), s_re[:, -1], s_im[:, -1]


def causal_conv(xb, buf, w, b):
    bsz, t, c = xb.shape
    if buf is None:
        buf = jnp.zeros((bsz, CONV_WIDTH - 1, c), xb.dtype)
    xp = jnp.concatenate([buf.astype(jnp.float32), xb.astype(jnp.float32)], axis=1)
    wf = w.astype(jnp.float32)
    y = b.astype(jnp.float32) + sum(xp[:, k:k + t] * wf[k] for k in range(CONV_WIDTH))
    return y, xp[:, -(CONV_WIDTH - 1):]


def _lin_combine(left, right):
    a1, b1 = left
    a2, b2 = right
    return a1 * a2, a2 * b1 + b2


def rglru(x, h0, w_a, b_a, w_i, b_i, lam):
    bsz, t, _ = x.shape
    xh = x.reshape(bsz, t, LRU_HEADS, LRU_BLOCK)
    r = jax.nn.sigmoid(jnp.einsum("bthi,hij->bthj", xh, w_a.astype(jnp.float32)) + b_a.astype(jnp.float32))
    ig = jax.nn.sigmoid(jnp.einsum("bthi,hij->bthj", xh, w_i.astype(jnp.float32)) + b_i.astype(jnp.float32))
    r = r.reshape(bsz, t, D_RNN)
    ig = ig.reshape(bsz, t, D_RNN)
    log_a = -LRU_C * r * jax.nn.softplus(-lam.astype(jnp.float32))
    a = jnp.exp(log_a)
    bx = jnp.sqrt(-jnp.expm1(2.0 * log_a)) * (ig * x)
    if h0 is not None:
        bx = bx.at[:, 0].add(a[:, 0] * h0.astype(jnp.float32))
    _, hs = lax.associative_scan(_lin_combine, (a, bx), axis=1)
    return hs, hs[:, -1]


def lru_mixer(h, conv_buf, h0, w_in, conv_w, conv_b, w_a, b_a, w_i, b_i, lam, w_out):
    z = h @ w_in
    gate, xb = z[..., :D_RNN], z[..., D_RNN:]
    xc, new_buf = causal_conv(xb, conv_buf, conv_w, conv_b)
    hs, h_last = rglru(xc, h0, w_a, b_a, w_i, b_i, lam)
    y = (hs * jax.nn.gelu(gate.astype(jnp.float32), approximate=False)) @ w_out.astype(jnp.float32)
    return y.astype(h.dtype), h_last, new_buf


def peer_retrieve(hf, w_q, sub_keys):
    n = hf.shape[0]
    q = (hf @ w_q).astype(jnp.float32).reshape(n, PEER_HEADS, 2, D_HALF)
    s = jnp.einsum("nhsk,hsmk->nhsm", q, sub_keys.astype(jnp.float32))
    top_s, top_i = lax.top_k(s, TOPK)
    cand_s = (top_s[:, :, 0, :, None] + top_s[:, :, 1, None, :]).reshape(n, PEER_HEADS, TOPK * TOPK)
    cand_i = (top_i[:, :, 0, :, None] * N_KEYS + top_i[:, :, 1, None, :]).reshape(n, PEER_HEADS, TOPK * TOPK)
    best_s, pos = lax.top_k(cand_s, TOPK)
    idx = jnp.take_along_axis(cand_i, pos, axis=-1)
    g = jax.nn.softmax(best_s, axis=-1)
    return idx, g


def peer_ffn(h, w_q, sub_keys, u_emb, v_emb):
    bsz, t, d = h.shape
    n = bsz * t
    hf = h.reshape(n, d)
    idx, g = peer_retrieve(hf, w_q, sub_keys)
    blk = min(PEER_TOKEN_BLOCK, n)
    n_pad = -(-n // blk) * blk
    pad = n_pad - n
    hp = jnp.pad(hf, ((0, pad), (0, 0))).reshape(-1, blk, d)
    ip = jnp.pad(idx, ((0, pad), (0, 0), (0, 0))).reshape(-1, blk, PEER_HEADS, TOPK)
    gp = jnp.pad(g, ((0, pad), (0, 0), (0, 0))).reshape(-1, blk, PEER_HEADS, TOPK)

    def block(args):
        xb, ib, gb = args
        u = u_emb[ib].astype(jnp.float32)
        act = jnp.einsum("nd,nhkd->nhk", xb.astype(jnp.float32), u)
        wgt = jax.nn.gelu(act, approximate=False) * gb
        v = v_emb[ib].astype(jnp.float32)
        return jnp.einsum("nhk,nhkd->nd", wgt, v)

    out = lax.map(block, (hp, ip, gp))
    return out.reshape(n_pad, d)[:n].reshape(bsz, t, d).astype(h.dtype)


def setup_inputs(seed: int = 0) -> dict:
    key = jax.random.key(seed)
    ks = iter(jax.random.split(key, 48))
    f32 = jnp.float32

    def nrm(shape, scale):
        return scale * jax.random.normal(next(ks), shape, f32)

    n_idx = jnp.arange(S5_STATE, dtype=f32)
    s5_shape = (N_S5_LAYERS, S5_GROUPS, S5_STATE)
    u_r = jax.random.uniform(next(ks), (N_LRU_LAYERS, D_RNN), f32, LRU_MIN_RAD ** 2, LRU_MAX_RAD ** 2)
    return {
        "x_prompt": nrm((BATCH, SEQ, D_MODEL), 1.0),
        "x_sample": nrm((DEC_BATCH, DEC_SEQ, D_MODEL), 1.0),
        "state_s5_re": nrm((N_S5_LAYERS, DEC_BATCH, S5_GROUPS, S5_STATE), 0.1),
        "state_s5_im": nrm((N_S5_LAYERS, DEC_BATCH, S5_GROUPS, S5_STATE), 0.1),
        "state_lru_h": nrm((N_LRU_LAYERS, DEC_BATCH, D_RNN), 0.5),
        "state_lru_conv": nrm((N_LRU_LAYERS, DEC_BATCH, CONV_WIDTH - 1, D_RNN), 1.0),
        "norm_mix": 1.0 + nrm((DEPTH, D_MODEL), 0.02),
        "norm_ffn": 1.0 + nrm((DEPTH, D_MODEL), 0.02),
        "norm_final": 1.0 + nrm((D_MODEL,), 0.02),
        "s5_lam_re": -0.5 + nrm(s5_shape, 0.01),
        "s5_lam_im": jnp.pi * n_idx + nrm(s5_shape, 0.01),
        "s5_log_dt": jax.random.uniform(next(ks), (N_S5_LAYERS, S5_GROUPS), f32,
                                        math.log(S5_DT_MIN), math.log(S5_DT_MAX)),
        "s5_b_re": nrm((N_S5_LAYERS, S5_GROUPS, S5_STATE, S5_GROUP), (2 * S5_GROUP) ** -0.5),
        "s5_b_im": nrm((N_S5_LAYERS, S5_GROUPS, S5_STATE, S5_GROUP), (2 * S5_GROUP) ** -0.5),
        "s5_c_re": nrm((N_S5_LAYERS, S5_GROUPS, S5_GROUP, S5_STATE), S5_STATE ** -0.5),
        "s5_c_im": nrm((N_S5_LAYERS, S5_GROUPS, S5_GROUP, S5_STATE), S5_STATE ** -0.5),
        "s5_d": 1.0 + nrm((N_S5_LAYERS, D_MODEL), 0.1),
        "s5_w_glu": nrm((N_S5_LAYERS, D_MODEL, D_MODEL), D_MODEL ** -0.5),
        "s5_b_glu": nrm((N_S5_LAYERS, D_MODEL), 0.01),
        "lru_w_in": nrm((N_LRU_LAYERS, D_MODEL, 2 * D_RNN), D_MODEL ** -0.5),
        "lru_conv_w": nrm((N_LRU_LAYERS, CONV_WIDTH, D_RNN), CONV_WIDTH ** -0.5),
        "lru_conv_b": nrm((N_LRU_LAYERS, D_RNN), 0.01),
        "lru_w_a": nrm((N_LRU_LAYERS, LRU_HEADS, LRU_BLOCK, LRU_BLOCK), LRU_BLOCK ** -0.5),
        "lru_b_a": nrm((N_LRU_LAYERS, LRU_HEADS, LRU_BLOCK), 0.01),
        "lru_w_i": nrm((N_LRU_LAYERS, LRU_HEADS, LRU_BLOCK, LRU_BLOCK), LRU_BLOCK ** -0.5),
        "lru_b_i": nrm((N_LRU_LAYERS, LRU_HEADS, LRU_BLOCK), 0.01),
        "lru_lam": -jnp.log(u_r ** -0.5 - 1.0),
        "lru_w_out": nrm((N_LRU_LAYERS, D_RNN, D_MODEL), D_RNN ** -0.5),
        "peer_w_q": nrm((DEPTH, D_MODEL, PEER_HEADS * D_KEY), D_MODEL ** -0.5),
        "peer_sub_keys": nrm((DEPTH, PEER_HEADS, 2, N_KEYS, D_HALF), D_HALF ** -0.5),
        "peer_u": nrm((DEPTH, N_EXPERTS, D_MODEL), D_MODEL ** -0.5),
        "peer_v": nrm((DEPTH, N_EXPERTS, D_MODEL), PEER_HEADS ** -0.5),
    }


def reference(x_prompt, x_sample, state_s5_re, state_s5_im, state_lru_h, state_lru_conv,
              norm_mix, norm_ffn, norm_final,
              s5_lam_re, s5_lam_im, s5_log_dt, s5_b_re, s5_b_im, s5_c_re, s5_c_im, s5_d, s5_w_glu, s5_b_glu,
              lru_w_in, lru_conv_w, lru_conv_b, lru_w_a, lru_b_a, lru_w_i, lru_b_i, lru_lam, lru_w_out,
              peer_w_q, peer_sub_keys, peer_u, peer_v):
    yp, ys = x_prompt, x_sample
    s5_rp, s5_ip, s5_rs, s5_is = [], [], [], []
    lru_hp, lru_cp, lru_hs, lru_cs = [], [], [], []
    for i in range(DEPTH):
        j = i // N_MIXERS
        hp = rmsnorm(yp, norm_mix[i])
        hs = rmsnorm(ys, norm_mix[i])
        if i % N_MIXERS == 0:
            prm = (s5_lam_re[j], s5_lam_im[j], s5_log_dt[j], s5_b_re[j], s5_b_im[j],
                   s5_c_re[j], s5_c_im[j], s5_d[j], s5_w_glu[j], s5_b_glu[j])
            op, rp, ip_ = s5_mixer(hp, None, None, *prm)
            os_, rs, is_ = s5_mixer(hs, state_s5_re[j], state_s5_im[j], *prm)
            s5_rp.append(rp)
            s5_ip.append(ip_)
            s5_rs.append(rs)
            s5_is.append(is_)
        else:
            prm = (lru_w_in[j], lru_conv_w[j], lru_conv_b[j], lru_w_a[j], lru_b_a[j],
                   lru_w_i[j], lru_b_i[j], lru_lam[j], lru_w_out[j])
            op, hlp, cbp = lru_mixer(hp, None, None, *prm)
            os_, hls, cbs = lru_mixer(hs, state_lru_conv[j], state_lru_h[j], *prm)
            lru_hp.append(hlp)
            lru_cp.append(cbp)
            lru_hs.append(hls)
            lru_cs.append(cbs)
        yp = yp + op
        ys = ys + os_
        yp = yp + peer_ffn(rmsnorm(yp, norm_ffn[i]), peer_w_q[i], peer_sub_keys[i], peer_u[i], peer_v[i])
        ys = ys + peer_ffn(rmsnorm(ys, norm_ffn[i]), peer_w_q[i], peer_sub_keys[i], peer_u[i], peer_v[i])
    y_prompt = rmsnorm(yp, norm_final)
    y_sample = rmsnorm(ys, norm_final)
    return (y_prompt, y_sample,
            jnp.stack(s5_rp), jnp.stack(s5_ip), jnp.stack(s5_rs), jnp.stack(s5_is),
            jnp.stack(lru_hp), jnp.stack(lru_cp), jnp.stack(lru_hs), jnp.stack(lru_cs))
```

```python
import functools
import math

import jax
import jax.numpy as jnp
from jax import lax
from jax.experimental import pallas as pl
from jax.experimental.pallas import tpu as pltpu

F32 = jnp.float32
BF16 = jnp.bfloat16

D_MODEL = 2048
RMS_EPS = 1e-6
S5_GROUP = 16
S5_GROUPS = 128
S5_STATE = 64
S5_KB = 8
S5_KBW = 256
S5_SW = 1024
LRU_HEADS = 8
LRU_BLOCK = 256
CONV_WIDTH = 4
LRU_C = 8.0
PEER_HEADS = 8
N_KEYS = 128
N_EXPERTS = N_KEYS * N_KEYS
TOPK = 16
SUBLANES = 8
LANES = 128
VMEM_LIMIT_BYTES = 56 * 1024 * 1024

NEG_INF = float("-inf")


def _cparams(n_axes):
    return pltpu.CompilerParams(
        dimension_semantics=("arbitrary",) * n_axes,
        vmem_limit_bytes=VMEM_LIMIT_BYTES,
    )


def _resident(block_shape, index_map):
    return pl.BlockSpec(block_shape, index_map, pipeline_mode=pl.Buffered(1))


def _rmsnorm(x, g):
    ms = jnp.mean(x * x, axis=-1, keepdims=True)
    return x * lax.rsqrt(ms + RMS_EPS) * g


def _gelu(x):
    return 0.5 * x * (1.0 + lax.erf(x * (1.0 / math.sqrt(2.0))))


def _dot(a, b):
    return jnp.dot(a, b, preferred_element_type=F32)


def _dot_nt(a, b):
    return lax.dot_general(a, b, (((1,), (1,)), ((), ())), preferred_element_type=F32)


def _dot_tn(a, b):
    return lax.dot_general(a, b, (((0,), (0,)), ((), ())), preferred_element_type=F32)


def _s5_discretize_kernel(lre_ref, lim_ref, ldt_ref, bre_ref, bim_ref,
                          pre_ref, pim_ref, bbre_ref, bbim_ref):
    lr = lre_ref[...]
    li = lim_ref[...]
    dt = jnp.exp(ldt_ref[...])
    mag = jnp.exp(lr * dt)
    ab_re = mag * jnp.cos(li * dt)
    ab_im = mag * jnp.sin(li * dt)
    nr, ni = ab_re - 1.0, ab_im
    den = lr * lr + li * li
    f_re = (nr * lr + ni * li) / den
    f_im = (ni * lr - nr * li) / den
    for c in range(S5_GROUP):
        br = bre_ref[c]
        bi = bim_ref[c]
        bbre_ref[c] = f_re * br - f_im * bi
        bbim_ref[c] = f_re * bi + f_im * br
    p_re, p_im = ab_re, ab_im
    for k in range(SUBLANES):
        pre_ref[k] = p_re
        pim_ref[k] = p_im
        p_re, p_im = p_re * ab_re - p_im * ab_im, p_re * ab_im + p_im * ab_re


def _s5_discretize(lam_re, lam_im, log_dt, b_re, b_im):
    g, p = S5_GROUPS, S5_STATE
    b_re_t = jnp.transpose(b_re, (2, 0, 1))
    b_im_t = jnp.transpose(b_im, (2, 0, 1))
    out_shape = (
        jax.ShapeDtypeStruct((SUBLANES, g, p), F32),
        jax.ShapeDtypeStruct((SUBLANES, g, p), F32),
        jax.ShapeDtypeStruct((S5_GROUP, g, p), F32),
        jax.ShapeDtypeStruct((S5_GROUP, g, p), F32),
    )
    return pl.pallas_call(_s5_discretize_kernel, out_shape=out_shape, name="s5_discretize")(
        lam_re, lam_im, log_dt.reshape(g, 1), b_re_t, b_im_t)


def _s5_weights(pow_re, pow_im, bb_re, bb_im, c_re, c_im, seg):
    eye = jnp.eye(S5_GROUP, dtype=F32)

    def in_proj(bb):
        bb = bb.reshape(S5_GROUP, S5_KB, S5_GROUP, S5_STATE)
        w = jnp.einsum("ckgp,gh->kgchp", bb, eye)
        return w.reshape(S5_KB, S5_KBW, S5_SW)

    def out_proj(c):
        c = c.reshape(S5_KB, S5_GROUP, S5_GROUP, S5_STATE)
        w = jnp.einsum("kgcp,gh->kgphc", c, eye)
        return w.reshape(S5_KB, S5_SW, S5_KBW)

    wb = jnp.concatenate([in_proj(bb_re), in_proj(bb_im)], axis=2).astype(BF16)
    wc = jnp.concatenate([out_proj(c_re), out_proj(-c_im)], axis=1).astype(BF16)

    row = jnp.arange(SUBLANES)
    rseg = row if seg is None else row % seg

    def consts(pw):
        pw = pw.reshape(SUBLANES, S5_KB, S5_SW)
        steps = []
        for d in (1, 2, 4):
            m = (rseg >= d).astype(F32)
            steps.append(pw[d - 1][:, None, :] * m[None, :, None])
        carry = jnp.transpose(pw[rseg], (1, 0, 2))
        return jnp.stack(steps + [carry], axis=1)

    return wb, wc, consts(pow_re), consts(pow_im)


def _s5_scan_kernel(*refs, seg4):
    if seg4:
        (x_ref, g_ref, wb_ref, wc_ref, cre_ref, cim_ref, h0re_ref, h0im_ref,
         ymix_ref, sre_ref, sim_ref, hb_scr, bu_scr) = refs
    else:
        (x_ref, g_ref, wb_ref, wc_ref, cre_ref, cim_ref,
         ymix_ref, sre_ref, sim_ref, hb_scr, bu_scr, carry_scr) = refs
    tc = pl.program_id(1)
    kb = pl.program_id(2)
    rows = x_ref.shape[0]

    @pl.when(kb == 0)
    def _():
        hb = _rmsnorm(x_ref[...], g_ref[...]).astype(BF16)
        for j in range(S5_KB):
            hb_scr[j] = hb[:, j * S5_KBW:(j + 1) * S5_KBW]

    if not seg4:
        @pl.when(jnp.logical_and(tc == 0, kb == 0))
        def _():
            carry_scr[...] = jnp.zeros_like(carry_scr)

    bu_scr[...] = _dot(hb_scr[kb], wb_ref[0])

    def body(r, carry):
        row = pl.multiple_of(r * SUBLANES, SUBLANES)
        re = bu_scr[pl.ds(row, SUBLANES), 0:S5_SW]
        im = bu_scr[pl.ds(row, SUBLANES), S5_SW:2 * S5_SW]
        for idx, d in enumerate((1, 2, 4)):
            if seg4 and d == 4:
                continue
            ar = cre_ref[0, idx]
            ai = cim_ref[0, idx]
            sr = pltpu.roll(re, d, 0)
            si = pltpu.roll(im, d, 0)
            re, im = re + ar * sr - ai * si, im + ar * si + ai * sr
        pr = cre_ref[0, 3]
        pi = cim_ref[0, 3]
        if seg4:
            cr = h0re_ref[0, pl.ds(row, SUBLANES), :]
            ci = h0im_ref[0, pl.ds(row, SUBLANES), :]
        else:
            cr, ci = carry
        re, im = re + pr * cr - pi * ci, im + pr * ci + pi * cr
        bu_scr[pl.ds(row, SUBLANES), 0:S5_SW] = re
        bu_scr[pl.ds(row, SUBLANES), S5_SW:2 * S5_SW] = im
        if seg4:
            return carry
        return (jnp.broadcast_to(re[SUBLANES - 1:SUBLANES], (SUBLANES, S5_SW)),
                jnp.broadcast_to(im[SUBLANES - 1:SUBLANES], (SUBLANES, S5_SW)))

    if seg4:
        lax.fori_loop(0, rows // SUBLANES, body, 0)
        sre_ref[0] = bu_scr[:, 0:S5_SW]
        sim_ref[0] = bu_scr[:, S5_SW:2 * S5_SW]
    else:
        c0 = (carry_scr[kb, :, 0:S5_SW], carry_scr[kb, :, S5_SW:2 * S5_SW])
        cr, ci = lax.fori_loop(0, rows // SUBLANES, body, c0)
        carry_scr[kb, :, 0:S5_SW] = cr
        carry_scr[kb, :, S5_SW:2 * S5_SW] = ci
        sre_ref[0, kb] = cr
        sim_ref[0, kb] = ci

    ymix_ref[...] = _dot(bu_scr[...].astype(BF16), wc_ref[0])


def _s5_scan(x, g, wb, wc, cre, cim, *, n_seq, rows, h0=None):
    n = x.shape[0]
    seg4 = h0 is not None
    n_chunk = n // (n_seq * rows)
    grid = (n_seq, n_chunk, S5_KB)

    def row_blk(s, t, k):
        return s * n_chunk + t

    in_specs = [
        pl.BlockSpec((rows, D_MODEL), lambda s, t, k: (row_blk(s, t, k), 0)),
        pl.BlockSpec((1, D_MODEL), lambda s, t, k: (0, 0)),
        pl.BlockSpec((1, S5_KBW, 2 * S5_SW), lambda s, t, k: (k, 0, 0)),
        pl.BlockSpec((1, 2 * S5_SW, S5_KBW), lambda s, t, k: (k, 0, 0)),
        pl.BlockSpec((1, 4, SUBLANES, S5_SW), lambda s, t, k: (k, 0, 0, 0)),
        pl.BlockSpec((1, 4, SUBLANES, S5_SW), lambda s, t, k: (k, 0, 0, 0)),
    ]
    args = [x, g, wb, wc, cre, cim]
    scratch = [pltpu.VMEM((S5_KB, rows, S5_KBW), BF16), pltpu.VMEM((rows, 2 * S5_SW), F32)]
    if seg4:
        h0_spec = pl.BlockSpec((1, rows, S5_SW), lambda s, t, k: (k, row_blk(s, t, k), 0))
        in_specs += [h0_spec, h0_spec]
        args += [h0[0], h0[1]]
        st_shape = jax.ShapeDtypeStruct((S5_KB, n, S5_SW), F32)
        st_spec = pl.BlockSpec((1, rows, S5_SW), lambda s, t, k: (k, row_blk(s, t, k), 0))
    else:
        scratch.append(pltpu.VMEM((S5_KB, SUBLANES, 2 * S5_SW), F32))
        st_shape = jax.ShapeDtypeStruct((n_seq, S5_KB, SUBLANES, S5_SW), F32)
        st_spec = pl.BlockSpec((1, S5_KB, SUBLANES, S5_SW), lambda s, t, k: (s, 0, 0, 0))
    out_shape = (jax.ShapeDtypeStruct((n, D_MODEL), F32), st_shape, st_shape)
    out_specs = (
        pl.BlockSpec((rows, S5_KBW), lambda s, t, k: (row_blk(s, t, k), k)),
        st_spec, st_spec,
    )
    return pl.pallas_call(
        functools.partial(_s5_scan_kernel, seg4=seg4),
        grid=grid, in_specs=in_specs, out_specs=out_specs, out_shape=out_shape,
        scratch_shapes=scratch, compiler_params=_cparams(3),
        name="s5_scan_seg4" if seg4 else "s5_scan",
    )(*args)


def _s5_glu_kernel(x_ref, g_ref, ymix_ref, d_ref, w_ref, b_ref, o_ref):
    x = x_ref[...]
    h = _rmsnorm(x, g_ref[...])
    y = _gelu(ymix_ref[...] + d_ref[...] * h)
    z = _dot(y.astype(BF16), w_ref[...]) + b_ref[...]
    o_ref[...] = x + y * jax.nn.sigmoid(z)


def _s5_glu(x, g, ymix, d_skip, w_glu, b_glu, *, tm):
    n = x.shape[0]
    tile = pl.BlockSpec((tm, D_MODEL), lambda i: (i, 0))
    vec = pl.BlockSpec((1, D_MODEL), lambda i: (0, 0))
    return pl.pallas_call(
        _s5_glu_kernel, grid=(n // tm,),
        in_specs=[tile, vec, tile, vec, _resident((D_MODEL, D_MODEL), lambda i: (0, 0)), vec],
        out_specs=tile, out_shape=jax.ShapeDtypeStruct((n, D_MODEL), F32),
        compiler_params=_cparams(1), name="s5_glu",
    )(x, g, ymix, d_skip, w_glu, b_glu)


def _matmul_kernel(*refs, norm, residual):
    refs = list(refs)
    x_ref = refs.pop(0)
    g_ref = refs.pop(0) if norm else None
    w_ref = refs.pop(0)
    r_ref = refs.pop(0) if residual else None
    o_ref = refs.pop(0)
    x = x_ref[...]
    if norm:
        x = _rmsnorm(x, g_ref[...]).astype(BF16)
    acc = _dot(x, w_ref[...])
    if residual:
        acc = acc + r_ref[...]
    o_ref[...] = acc


def _matmul(x, w, *, tm, tn, g=None, res=None, name):
    n, k = x.shape
    n_out = w.shape[1]
    grid = (n_out // tn, n // tm)
    in_specs = [pl.BlockSpec((tm, k), lambda j, i: (i, 0))]
    args = [x]
    if g is not None:
        in_specs.append(pl.BlockSpec((1, k), lambda j, i: (0, 0)))
        args.append(g)
    in_specs.append(pl.BlockSpec((k, tn), lambda j, i: (0, j)))
    args.append(w)
    if res is not None:
        in_specs.append(pl.BlockSpec((tm, tn), lambda j, i: (i, j)))
        args.append(res)
    return pl.pallas_call(
        functools.partial(_matmul_kernel, norm=g is not None, residual=res is not None),
        grid=grid, in_specs=in_specs,
        out_specs=pl.BlockSpec((tm, tn), lambda j, i: (i, j)),
        out_shape=jax.ShapeDtypeStruct((n, n_out), F32),
        compiler_params=_cparams(2), name=name,
    )(*args)


def _lru_kernel(*refs, seg4):
    if seg4:
        (z_ref, cw_ref, cb_ref, wa_ref, ba_ref, wi_ref, bi_ref, lam_ref, pcv_ref, h0_ref,
         gated_ref, hst_ref, xbst_ref, a_scr, b_scr) = refs
    else:
        (z_ref, cw_ref, cb_ref, wa_ref, ba_ref, wi_ref, bi_ref, lam_ref,
         gated_ref, hst_ref, xbst_ref, a_scr, b_scr, prev_scr, carry_scr) = refs
    tc = pl.program_id(1)
    rows = z_ref.shape[0]
    d = D_MODEL
    gate = z_ref[:, 0:d]
    xb = z_ref[:, d:2 * d]

    row8 = lax.broadcasted_iota(jnp.int32, (SUBLANES, d), 0)
    if seg4:
        t_full = lax.broadcasted_iota(jnp.int32, (rows, d), 0) % CONV_WIDTH
        xc = cb_ref[...] + cw_ref[3:4, :] * xb
        for s in range(1, CONV_WIDTH):
            shifted = jnp.where(t_full >= s, pltpu.roll(xb, s, 0), pcv_ref[s - 1])
            xc = xc + cw_ref[3 - s:4 - s, :] * shifted
        xbst_ref[...] = xb
        rseg = row8 % CONV_WIDTH
    else:
        @pl.when(tc == 0)
        def _():
            prev_scr[...] = jnp.zeros_like(prev_scr)
            carry_scr[...] = jnp.zeros_like(carry_scr)
        xcat = jnp.concatenate([prev_scr[...], xb], axis=0)
        xc = cb_ref[...] + cw_ref[3:4, :] * xb
        for s in range(1, CONV_WIDTH):
            xc = xc + cw_ref[3 - s:4 - s, :] * pltpu.roll(xcat, s, 0)[SUBLANES:]
        tail = xb[rows - SUBLANES:]
        prev_scr[...] = tail
        xbst_ref[0] = tail
        rseg = row8

    r_parts, i_parts = [], []
    for hh in range(LRU_HEADS):
        xh = xc[:, hh * LRU_BLOCK:(hh + 1) * LRU_BLOCK].astype(BF16)
        r_parts.append(_dot(xh, wa_ref[hh]))
        i_parts.append(_dot(xh, wi_ref[hh]))
    r = jax.nn.sigmoid(jnp.concatenate(r_parts, axis=1) + ba_ref[...])
    ig = jax.nn.sigmoid(jnp.concatenate(i_parts, axis=1) + bi_ref[...])
    log_a = -LRU_C * r * jax.nn.softplus(-lam_ref[...])
    a = jnp.exp(log_a)
    a_scr[...] = a
    b_scr[...] = jnp.sqrt(-jnp.tanh(log_a) * (a * a + 1.0)) * (ig * xc)

    masks = [rseg >= s for s in (1, 2, 4)]

    def body(g, carry):
        row = pl.multiple_of(g * SUBLANES, SUBLANES)
        av = a_scr[pl.ds(row, SUBLANES), :]
        bv = b_scr[pl.ds(row, SUBLANES), :]
        for m, s in zip(masks, (1, 2, 4)):
            if seg4 and s == 4:
                continue
            a_sh = jnp.where(m, pltpu.roll(av, s, 0), 1.0)
            b_sh = jnp.where(m, pltpu.roll(bv, s, 0), 0.0)
            bv = bv + av * b_sh
            av = av * a_sh
        c = h0_ref[pl.ds(row, SUBLANES), :] if seg4 else carry
        hv = bv + av * c
        b_scr[pl.ds(row, SUBLANES), :] = hv
        if seg4:
            return carry
        return jnp.broadcast_to(hv[SUBLANES - 1:SUBLANES], (SUBLANES, d))

    if seg4:
        lax.fori_loop(0, rows // SUBLANES, body, 0)
        hst_ref[...] = b_scr[...]
    else:
        c = lax.fori_loop(0, rows // SUBLANES, body, carry_scr[...])
        carry_scr[...] = c
        hst_ref[0] = c
    gated_ref[...] = (b_scr[...] * _gelu(gate)).astype(BF16)


def _lru(z, conv_w, conv_b, w_a, b_a, w_i, b_i, lam, *, n_seq, rows, pcv=None, h0=None):
    n = z.shape[0]
    seg4 = h0 is not None
    d = D_MODEL
    n_chunk = n // (n_seq * rows)

    def row_map(s, t):
        return (s * n_chunk + t, 0)

    vec = pl.BlockSpec((1, d), lambda s, t: (0, 0))
    gate_w = pl.BlockSpec((LRU_HEADS, LRU_BLOCK, LRU_BLOCK), lambda s, t: (0, 0, 0))
    in_specs = [pl.BlockSpec((rows, 2 * d), row_map),
                pl.BlockSpec((CONV_WIDTH, d), lambda s, t: (0, 0)), vec,
                gate_w, vec, gate_w, vec, vec]
    args = [z, conv_w, conv_b, w_a, b_a, w_i, b_i, lam]
    scratch = [pltpu.VMEM((rows, d), F32), pltpu.VMEM((rows, d), F32)]
    tile = pl.BlockSpec((rows, d), row_map)
    if seg4:
        in_specs += [pl.BlockSpec((CONV_WIDTH - 1, rows, d), lambda s, t: (0, s * n_chunk + t, 0)), tile]
        args += [pcv, h0]
        st_shape = jax.ShapeDtypeStruct((n, d), F32)
        st_spec = tile
    else:
        scratch += [pltpu.VMEM((SUBLANES, d), F32), pltpu.VMEM((SUBLANES, d), F32)]
        st_shape = jax.ShapeDtypeStruct((n_seq, SUBLANES, d), F32)
        st_spec = pl.BlockSpec((1, SUBLANES, d), lambda s, t: (s, 0, 0))
    return pl.pallas_call(
        functools.partial(_lru_kernel, seg4=seg4),
        grid=(n_seq, n_chunk), in_specs=in_specs,
        out_specs=(tile, st_spec, st_spec),
        out_shape=(jax.ShapeDtypeStruct((n, d), BF16), st_shape, st_shape),
        scratch_shapes=scratch, compiler_params=_cparams(2),
        name="lru_seg4" if seg4 else "lru",
    )(*args)


def _top_rows(vals_scr, top_scr, n_rows):
    def body(r, carry):
        v = vals_scr[0:n_rows]
        m = jnp.max(v, axis=0, keepdims=True)
        top_scr[pl.ds(r, 1), :] = m
        vals_scr[0:n_rows] = jnp.where(v == m, NEG_INF, v)
        return carry
    lax.fori_loop(0, TOPK + 1, body, 0)


def _peer_retrieve_kernel(y_ref, g_ref, wq_ref, keys_ref,
                          hb_ref, th_ref, rr_ref, s2_ref, p2_ref,
                          vals_scr, ta_scr, tb_scr, cand_scr, top_scr):
    tm = y_ref.shape[0]
    hb = _rmsnorm(y_ref[...], g_ref[...]).astype(BF16)
    hb_ref[...] = hb
    qb = _dot(hb, wq_ref[...]).astype(BF16)
    row8 = lax.broadcasted_iota(jnp.int32, (SUBLANES, tm), 0)
    for hh in range(PEER_HEADS):
        c0 = hh * 2 * N_KEYS
        s1 = _dot_nt(keys_ref[hh, 0], qb[:, c0:c0 + N_KEYS])
        s2 = _dot_nt(keys_ref[hh, 1], qb[:, c0 + N_KEYS:c0 + 2 * N_KEYS])
        vals_scr[...] = s1
        _top_rows(vals_scr, ta_scr, N_KEYS)
        vals_scr[...] = s2
        _top_rows(vals_scr, tb_scr, N_KEYS)
        a = ta_scr[0:TOPK]
        b = tb_scr[0:TOPK]
        a16 = ta_scr[TOPK:TOPK + 1]
        b16 = tb_scr[TOPK:TOPK + 1]
        a0, b0 = a[0:1], b[0:1]
        a_lo, b_lo = a[0:SUBLANES], b[0:SUBLANES]
        cand_scr[0:16] = a0 + b
        cand_scr[16:24] = a[1:2] + b_lo
        cand_scr[24:32] = jnp.where(row8 < 5, a[2:3] + b_lo, NEG_INF)
        cand_scr[32:40] = jnp.where(row8 < 4, a[3:4] + b_lo, NEG_INF)
        cand_scr[40:48] = jnp.where(row8 < 3, a[4:5] + b_lo, NEG_INF)
        cand_scr[48:56] = a[SUBLANES:2 * SUBLANES] + b0
        rank17 = jnp.where(row8 == 0, a16 + b0, jnp.where(row8 == 1, a0 + b16, NEG_INF))
        cand_scr[56:64] = jnp.where(row8 >= 5, a_lo + b0, rank17)
        cand_scr[64:72] = jnp.where(row8 >= 5, a_lo + b[1:2], NEG_INF)
        cand = cand_scr[...]
        vals_scr[0:72] = cand
        _top_rows(vals_scr, top_scr, 72)
        tau = 0.5 * (top_scr[TOPK - 1:TOPK] + top_scr[TOPK:TOPK + 1])
        z =jnp.sum(jnp.where(cand >= tau, jnp.exp(cand - (a0 + b0)), 0.0), axis=0, keepdims=True)
        th_ref[hh] = tau - s1
        rr_ref[hh] = jnp.exp(s1 - a0 - jnp.log(z))
        s2_ref[hh] = s2
        p2_ref[hh] = jnp.exp(s2 - b0)


def _peer_retrieve(y, g, w_q, keys, *, tm):
    n = y.shape[0]
    fac_shape = jax.ShapeDtypeStruct((PEER_HEADS, N_KEYS, n), F32)
    fac_spec = pl.BlockSpec((PEER_HEADS, N_KEYS, tm), lambda i: (0, 0, i))
    return pl.pallas_call(
        _peer_retrieve_kernel, grid=(n // tm,),
        in_specs=[pl.BlockSpec((tm, D_MODEL), lambda i: (i, 0)),
                  pl.BlockSpec((1, D_MODEL), lambda i: (0, 0)),
                  _resident((D_MODEL, D_MODEL), lambda i: (0, 0)),
                  pl.BlockSpec((PEER_HEADS, 2, N_KEYS, N_KEYS), lambda i: (0, 0, 0, 0))],
        out_specs=(pl.BlockSpec((tm, D_MODEL), lambda i: (i, 0)),
                   fac_spec, fac_spec, fac_spec, fac_spec),
        out_shape=(jax.ShapeDtypeStruct((n, D_MODEL), BF16),
                   fac_shape, fac_shape, fac_shape, fac_shape),
        scratch_shapes=[pltpu.VMEM((N_KEYS, tm), F32), pltpu.VMEM((24, tm), F32),
                        pltpu.VMEM((24, tm), F32), pltpu.VMEM((72, tm), F32),
                        pltpu.VMEM((24, tm), F32)],
        compiler_params=_cparams(1), name="peer_retrieve",
    )(y, g, w_q, keys)


def _peer_dense_kernel(*refs, eb, final_norm):
    if final_norm:
        (hb_ref, y_ref, u_ref, v_ref, th_ref, rr_ref, s2_ref, p2_ref, gf_ref, o_ref) = refs
    else:
        (hb_ref, y_ref, u_ref, v_ref, th_ref, rr_ref, s2_ref, p2_ref, o_ref) = refs
    e = pl.program_id(1)
    n_sub = eb // N_KEYS

    @pl.when(e == 0)
    def _():
        o_ref[...] = y_ref[...]

    act = _gelu(_dot_nt(u_ref[...], hb_ref[...]))
    parts = []
    for i in range(n_sub):
        blk = e * n_sub + i
        gate = jnp.zeros((N_KEYS, act.shape[1]), F32)
        for hh in range(PEER_HEADS):
            th = th_ref[hh, pl.ds(blk, 1), :]
            rr = rr_ref[hh, pl.ds(blk, 1), :]
            gate = gate + jnp.where(s2_ref[hh] >= th, p2_ref[hh] * rr, 0.0)
        parts.append(act[i * N_KEYS:(i + 1) * N_KEYS] * gate)
    w_t = jnp.concatenate(parts, axis=0).astype(BF16)
    o_ref[...] += _dot_tn(w_t, v_ref[...])

    if final_norm:
        @pl.when(e == pl.num_programs(1) - 1)
        def _():
            o_ref[...] = _rmsnorm(o_ref[...], gf_ref[...])


def _peer_dense(hb, y, u, v, th, rr, s2, p2, *, tm, eb, g_final=None):
    n = y.shape[0]
    tile = pl.BlockSpec((tm, D_MODEL), lambda i, e: (i, 0))
    table = pl.BlockSpec((eb, D_MODEL), lambda i, e: (e, 0))
    fac = pl.BlockSpec((PEER_HEADS, N_KEYS, tm), lambda i, e: (0, 0, i))
    in_specs = [tile, tile, table, table, fac, fac, fac, fac]
    args = [hb, y, u, v, th, rr, s2, p2]
    if g_final is not None:
        in_specs.append(pl.BlockSpec((1, D_MODEL), lambda i, e: (0, 0)))
        args.append(g_final)
    return pl.pallas_call(
        functools.partial(_peer_dense_kernel, eb=eb, final_norm=g_final is not None),
        grid=(n // tm, N_EXPERTS // eb), in_specs=in_specs, out_specs=tile,
        out_shape=jax.ShapeDtypeStruct((n, D_MODEL), F32),
        compiler_params=_cparams(2), name="peer_dense",
    )(*args)


def _peer(y, g, w_q, keys, u, v, *, tm, eb, g_final=None):
    hb, th, rr, s2, p2 = _peer_retrieve(y, g, w_q, keys, tm=tm)
    return _peer_dense(hb, y, u, v, th, rr, s2, p2, tm=tm, eb=eb, g_final=g_final)


PEER_TM = 512
PEER_EB = 512
SEQ_ROWS = 256


def _row(v):
    return v.reshape(1, -1)


@jax.jit
def _step(x_prompt, x_sample, state_s5_re, state_s5_im, state_lru_h, state_lru_conv,
          norm_mix, norm_ffn, norm_final,
          s5_lam_re, s5_lam_im, s5_log_dt, s5_b_re, s5_b_im, s5_c_re, s5_c_im, s5_d, s5_w_glu, s5_b_glu,
          lru_w_in, lru_conv_w, lru_conv_b, lru_w_a, lru_b_a, lru_w_i, lru_b_i, lru_lam, lru_w_out,
          peer_w_q, peer_sub_keys, peer_u, peer_v):
    bsz, seq, d = x_prompt.shape
    dec_b, dec_t, _ = x_sample.shape
    xp = x_prompt.reshape(bsz * seq, d)
    xs = x_sample.reshape(dec_b * dec_t, d)
    n_s = dec_b * dec_t

    pow_re, pow_im, bb_re, bb_im = _s5_discretize(
        s5_lam_re[0], s5_lam_im[0], s5_log_dt[0], s5_b_re[0], s5_b_im[0])
    g_mix0 = _row(norm_mix[0])
    w_glu = s5_w_glu[0].astype(BF16)
    outs = []
    for mode, x in (("prompt", xp), ("sample", xs)):
        if mode == "prompt":
            wb, wc, cre, cim = _s5_weights(pow_re, pow_im, bb_re, bb_im, s5_c_re[0], s5_c_im[0], None)
            ymix, sre, sim = _s5_scan(x, g_mix0, wb, wc, cre, cim, n_seq=bsz, rows=SEQ_ROWS)
            st = (sre[:, :, 0, :].reshape(1, bsz, S5_GROUPS, S5_STATE),
                  sim[:, :, 0, :].reshape(1, bsz, S5_GROUPS, S5_STATE))
        else:
            wb, wc, cre, cim = _s5_weights(pow_re, pow_im, bb_re, bb_im, s5_c_re[0], s5_c_im[0], dec_t)

            def rep(s):
                s = jnp.transpose(s.reshape(dec_b, S5_KB, S5_SW), (1, 0, 2))
                return jnp.repeat(s, dec_t, axis=1)
            ymix, sre, sim = _s5_scan(x, g_mix0, wb, wc, cre, cim, n_seq=n_s // SEQ_ROWS, rows=SEQ_ROWS,
                                      h0=(rep(state_s5_re[0]), rep(state_s5_im[0])))

            def last(s):
                s = s.reshape(S5_KB, dec_b, dec_t, S5_SW)[:, :, dec_t - 1, :]
                return jnp.transpose(s, (1, 0, 2)).reshape(1, dec_b, S5_GROUPS, S5_STATE)
            st = (last(sre), last(sim))
        y = _s5_glu(x, g_mix0, ymix, _row(s5_d[0]), w_glu, _row(s5_b_glu[0]), tm=PEER_TM)
        outs.append((y, st))
    (yp, s5_p), (ys, s5_s) = outs

    def peer_layer(i, y, g_final=None):
        return _peer(y, _row(norm_ffn[i]), peer_w_q[i].astype(BF16), peer_sub_keys[i].astype(BF16),
                     peer_u[i].astype(BF16), peer_v[i].astype(BF16),
                     tm=PEER_TM, eb=PEER_EB, g_final=g_final)
    yp = peer_layer(0, yp)
    ys = peer_layer(0, ys)

    g_mix1 = _row(norm_mix[1])
    w_in = lru_w_in[0].astype(BF16)
    w_out = lru_w_out[0].astype(BF16)
    gate_args = (lru_conv_w[0], _row(lru_conv_b[0]), lru_w_a[0].astype(BF16), _row(lru_b_a[0]),
                 lru_w_i[0].astype(BF16), _row(lru_b_i[0]), _row(lru_lam[0]))

    zp = _matmul(yp, w_in, tm=PEER_TM, tn=D_MODEL, g=g_mix1, name="lru_in_proj")
    gated_p, hst_p, xbst_p = _lru(zp, *gate_args, n_seq=bsz, rows=SEQ_ROWS)
    yp = _matmul(gated_p, w_out, tm=PEER_TM, tn=D_MODEL, res=yp, name="lru_out_proj")
    lru_h_p = hst_p[:, 0, :].reshape(1, bsz, d)
    lru_c_p = xbst_p[:, SUBLANES - (CONV_WIDTH - 1):, :].reshape(1, bsz, CONV_WIDTH - 1, d)

    buf = state_lru_conv[0]
    zero = jnp.zeros((dec_b, 1, d), F32)
    pcv = jnp.stack([
        jnp.concatenate([buf[:, 2:3], zero, zero, zero], axis=1),
        jnp.concatenate([buf[:, 1:3], zero, zero], axis=1),
        jnp.concatenate([buf[:, 0:3], zero], axis=1),
    ]).reshape(CONV_WIDTH - 1, n_s, d)
    h0 = jnp.repeat(state_lru_h[0], dec_t, axis=0)
    zs = _matmul(ys, w_in, tm=PEER_TM, tn=D_MODEL, g=g_mix1, name="lru_in_proj")
    gated_s, hst_s, xbst_s = _lru(zs, *gate_args, n_seq=n_s // SEQ_ROWS, rows=SEQ_ROWS, pcv=pcv, h0=h0)
    ys = _matmul(gated_s, w_out, tm=PEER_TM, tn=D_MODEL, res=ys, name="lru_out_proj")
    lru_h_s = hst_s.reshape(dec_b, dec_t, d)[:, dec_t - 1].reshape(1, dec_b, d)
    lru_c_s = xbst_s.reshape(dec_b, dec_t, d)[:, 1:].reshape(1, dec_b, CONV_WIDTH - 1, d)

    g_fin = _row(norm_final)
    yp = peer_layer(1, yp, g_final=g_fin)
    ys = peer_layer(1, ys, g_final=g_fin)

    return (yp.reshape(bsz, seq, d), ys.reshape(dec_b, dec_t, d),
            s5_p[0], s5_p[1], s5_s[0], s5_s[1],
            lru_h_p, lru_c_p, lru_h_s, lru_c_s)


def kernel(x_prompt, x_sample, state_s5_re, state_s5_im, state_lru_h, state_lru_conv, norm_mix, norm_ffn, norm_final, s5_lam_re, s5_lam_im, s5_log_dt, s5_b_re, s5_b_im, s5_c_re, s5_c_im, s5_d, s5_w_glu, s5_b_glu, lru_w_in, lru_conv_w, lru_conv_b, lru_w_a, lru_b_a, lru_w_i, lru_b_i, lru_lam, lru_w_out, peer_w_q, peer_sub_keys, peer_u, peer_v):
    return _step(x_prompt, x_sample, state_s5_re, state_s5_im, state_lru_h, state_lru_conv,
                 norm_mix, norm_ffn, norm_final,
                 s5_lam_re, s5_lam_im, s5_log_dt, s5_b_re, s5_b_im, s5_c_re, s5_c_im, s5_d, s5_w_glu, s5_b_glu,
                 lru_w_in, lru_conv_w, lru_conv_b, lru_w_a, lru_b_a, lru_w_i, lru_b_i, lru_lam, lru_w_out,
                 peer_w_q, peer_sub_keys, peer_u, peer_v)
```

```python
import functools
import math

import jax
import jax.numpy as jnp
from jax import lax
from jax.experimental import pallas as pl
from jax.experimental.pallas import tpu as pltpu

F32 = jnp.float32
BF16 = jnp.bfloat16

D_MODEL = 2048
RMS_EPS = 1e-6
S5_GROUP = 16
S5_GROUPS = 128
S5_STATE = 64
S5_KB = 8
S5_KBW = 256
S5_SW = 1024
LRU_HEADS = 8
LRU_BLOCK = 256
CONV_WIDTH = 4
LRU_C = 8.0
PEER_HEADS = 8
N_KEYS = 128
N_EXPERTS = N_KEYS * N_KEYS
TOPK = 16
SUBLANES = 8
LANES = 128
BF16_ROWS = 2 * SUBLANES
VMEM_LIMIT_BYTES = 56 * 1024 * 1024

NEG_INF = float("-inf")


def _cparams(n_axes, flags=None):
    return pltpu.CompilerParams(
        dimension_semantics=("arbitrary",) * n_axes,
        vmem_limit_bytes=VMEM_LIMIT_BYTES,
        flags=flags,
    )


def _resident(block_shape, index_map):
    return pl.BlockSpec(block_shape, index_map, pipeline_mode=pl.Buffered(1))


def _rmsnorm(x, g):
    ms = jnp.mean(x * x, axis=-1, keepdims=True)
    return x * lax.rsqrt(ms + RMS_EPS) * g


def _gelu(x):
    return 0.5 * x * (1.0 + lax.erf(x * (1.0 / math.sqrt(2.0))))


def _dot(a, b):
    return jnp.dot(a, b, preferred_element_type=F32)


def _dot_nt(a, b):
    return lax.dot_general(a, b, (((1,), (1,)), ((), ())), preferred_element_type=F32)


def _dot_tn(a, b):
    return lax.dot_general(a, b, (((0,), (0,)), ((), ())), preferred_element_type=F32)


def _cast_kernel(x_ref, o_ref):
    o_ref[...] = x_ref[...].astype(BF16)


def _cast_bf16(x, *, rows):
    n, d = x.shape
    spec = pl.BlockSpec((rows, d), lambda i: (i, 0))
    return pl.pallas_call(
        _cast_kernel, grid=(n // rows,), in_specs=[spec], out_specs=spec,
        out_shape=jax.ShapeDtypeStruct((n, d), BF16),
        compiler_params=_cparams(1), name="cast_bf16",
    )(x)


def _s5_discretize_kernel(lre_ref, lim_ref, ldt_ref, bre_ref, bim_ref,
                          pre_ref, pim_ref, bbre_ref, bbim_ref):
    lr = lre_ref[...]
    li = lim_ref[...]
    dt = jnp.exp(ldt_ref[...])
    mag = jnp.exp(lr * dt)
    ab_re = mag * jnp.cos(li * dt)
    ab_im = mag * jnp.sin(li * dt)
    nr, ni = ab_re - 1.0, ab_im
    den = lr * lr + li * li
    f_re = (nr * lr + ni * li) / den
    f_im = (ni * lr - nr * li) / den
    for c in range(S5_GROUP):
        br = bre_ref[c]
        bi = bim_ref[c]
        bbre_ref[c] = f_re * br - f_im * bi
        bbim_ref[c] = f_re * bi + f_im * br
    p_re, p_im = ab_re, ab_im
    for k in range(SUBLANES):
        pre_ref[k] = p_re
        pim_ref[k] = p_im
        p_re, p_im = p_re * ab_re - p_im * ab_im, p_re * ab_im + p_im * ab_re


def _s5_discretize(lam_re, lam_im, log_dt, b_re, b_im):
    g, p = S5_GROUPS, S5_STATE
    b_re_t = jnp.transpose(b_re, (2, 0, 1))
    b_im_t = jnp.transpose(b_im, (2, 0, 1))
    out_shape = (
        jax.ShapeDtypeStruct((SUBLANES, g, p), F32),
        jax.ShapeDtypeStruct((SUBLANES, g, p), F32),
        jax.ShapeDtypeStruct((S5_GROUP, g, p), F32),
        jax.ShapeDtypeStruct((S5_GROUP, g, p), F32),
    )
    return pl.pallas_call(_s5_discretize_kernel, out_shape=out_shape, name="s5_discretize")(
        lam_re, lam_im, log_dt.reshape(g, 1), b_re_t, b_im_t)


def _s5_weights(pow_re, pow_im, bb_re, bb_im, c_re, c_im, seg):
    eye = jnp.eye(S5_GROUP, dtype=F32)

    def in_proj(bb):
        bb = bb.reshape(S5_GROUP, S5_KB, S5_GROUP, S5_STATE)
        w = jnp.einsum("ckgp,gh->kgchp", bb, eye)
        return w.reshape(S5_KB, S5_KBW, S5_SW)

    def out_proj(c):
        c = c.reshape(S5_KB, S5_GROUP, S5_GROUP, S5_STATE)
        w = jnp.einsum("kgcp,gh->kgphc", c, eye)
        return w.reshape(S5_KB, S5_SW, S5_KBW)

    wb = jnp.concatenate([in_proj(bb_re), in_proj(bb_im)], axis=2).astype(BF16)
    wc = jnp.concatenate([out_proj(c_re), out_proj(-c_im)], axis=1).astype(BF16)

    row = jnp.arange(SUBLANES)
    rseg = row if seg is None else row % seg

    def consts(pw):
        pw = pw.reshape(SUBLANES, S5_KB, S5_SW)
        steps = []
        for d in (1, 2, 4):
            m = (rseg >= d).astype(F32)
            steps.append(pw[d - 1][:, None, :] * m[None, :, None])
        carry = jnp.transpose(pw[rseg], (1, 0, 2))
        return jnp.stack(steps + [carry], axis=1)

    return wb, wc, consts(pow_re), consts(pow_im)


def _s5_scan_kernel(*refs, seg4):
    if seg4:
        (x_ref, g_ref, wb_ref, wc_ref, cre_ref, cim_ref, h0re_ref, h0im_ref,
         ymix_ref, sre_ref, sim_ref, hb_scr, bu_scr) = refs
    else:
        (x_ref, g_ref, wb_ref, wc_ref, cre_ref, cim_ref,
         ymix_ref, sre_ref, sim_ref, hb_scr, bu_scr, carry_scr) = refs
    tc = pl.program_id(1)
    kb = pl.program_id(2)
    rows = x_ref.shape[0]

    @pl.when(kb == 0)
    def _():
        hb = _rmsnorm(x_ref[...], g_ref[...]).astype(BF16)
        for j in range(S5_KB):
            hb_scr[j] = hb[:, j * S5_KBW:(j + 1) * S5_KBW]

    if not seg4:
        @pl.when(jnp.logical_and(tc == 0, kb == 0))
        def _():
            carry_scr[...] = jnp.zeros_like(carry_scr)

    bu_scr[...] = _dot(hb_scr[kb], wb_ref[0])

    def body(r, carry):
        row = pl.multiple_of(r * SUBLANES, SUBLANES)
        re = bu_scr[pl.ds(row, SUBLANES), 0:S5_SW]
        im = bu_scr[pl.ds(row, SUBLANES), S5_SW:2 * S5_SW]
        for idx, d in enumerate((1, 2, 4)):
            if seg4 and d == 4:
                continue
            ar = cre_ref[0, idx]
            ai = cim_ref[0, idx]
            sr = pltpu.roll(re, d, 0)
            si = pltpu.roll(im, d, 0)
            re, im = re + ar * sr - ai * si, im + ar * si + ai * sr
        pr = cre_ref[0, 3]
        pi = cim_ref[0, 3]
        if seg4:
            cr = h0re_ref[0, pl.ds(row, SUBLANES), :]
            ci = h0im_ref[0, pl.ds(row, SUBLANES), :]
        else:
            cr, ci = carry
        re, im = re + pr * cr - pi * ci, im + pr * ci + pi * cr
        bu_scr[pl.ds(row, SUBLANES), 0:S5_SW] = re
        bu_scr[pl.ds(row, SUBLANES), S5_SW:2 * S5_SW] = im
        if seg4:
            return carry
        return (jnp.broadcast_to(re[SUBLANES - 1:SUBLANES], (SUBLANES, S5_SW)),
                jnp.broadcast_to(im[SUBLANES - 1:SUBLANES], (SUBLANES, S5_SW)))

    if seg4:
        lax.fori_loop(0, rows // SUBLANES, body, 0)
        sre_ref[0] = bu_scr[:, 0:S5_SW]
        sim_ref[0] = bu_scr[:, S5_SW:2 * S5_SW]
    else:
        c0 = (carry_scr[kb, :, 0:S5_SW], carry_scr[kb, :, S5_SW:2 * S5_SW])
        cr, ci = lax.fori_loop(0, rows // SUBLANES, body, c0)
        carry_scr[kb, :, 0:S5_SW] = cr
        carry_scr[kb, :, S5_SW:2 * S5_SW] = ci
        sre_ref[0, kb] = cr
        sim_ref[0, kb] = ci

    ymix_ref[...] = _dot(bu_scr[...].astype(BF16), wc_ref[0])


def _s5_scan(x, g, wb, wc, cre, cim, *, n_seq, rows, h0=None):
    n = x.shape[0]
    seg4 = h0 is not None
    n_chunk = n // (n_seq * rows)
    grid = (n_seq, n_chunk, S5_KB)

    def row_blk(s, t, k):
        return s * n_chunk + t

    in_specs = [
        pl.BlockSpec((rows, D_MODEL), lambda s, t, k: (row_blk(s, t, k), 0)),
        pl.BlockSpec((1, D_MODEL), lambda s, t, k: (0, 0)),
        pl.BlockSpec((1, S5_KBW, 2 * S5_SW), lambda s, t, k: (k, 0, 0)),
        pl.BlockSpec((1, 2 * S5_SW, S5_KBW), lambda s, t, k: (k, 0, 0)),
        pl.BlockSpec((1, 4, SUBLANES, S5_SW), lambda s, t, k: (k, 0, 0, 0)),
        pl.BlockSpec((1, 4, SUBLANES, S5_SW), lambda s, t, k: (k, 0, 0, 0)),
    ]
    args = [x, g, wb, wc, cre, cim]
    scratch = [pltpu.VMEM((S5_KB, rows, S5_KBW), BF16), pltpu.VMEM((rows, 2 * S5_SW), F32)]
    if seg4:
        h0_spec = pl.BlockSpec((1, rows, S5_SW), lambda s, t, k: (k, row_blk(s, t, k), 0))
        in_specs += [h0_spec, h0_spec]
        args += [h0[0], h0[1]]
        st_shape = jax.ShapeDtypeStruct((S5_KB, n, S5_SW), F32)
        st_spec = pl.BlockSpec((1, rows, S5_SW), lambda s, t, k: (k, row_blk(s, t, k), 0))
    else:
        scratch.append(pltpu.VMEM((S5_KB, SUBLANES, 2 * S5_SW), F32))
        st_shape = jax.ShapeDtypeStruct((n_seq, S5_KB, SUBLANES, S5_SW), F32)
        st_spec = pl.BlockSpec((1, S5_KB, SUBLANES, S5_SW), lambda s, t, k: (s, 0, 0, 0))
    out_shape = (jax.ShapeDtypeStruct((n, D_MODEL), F32), st_shape, st_shape)
    out_specs = (
        pl.BlockSpec((rows, S5_KBW), lambda s, t, k: (row_blk(s, t, k), k)),
        st_spec, st_spec,
    )
    return pl.pallas_call(
        functools.partial(_s5_scan_kernel, seg4=seg4),
        grid=grid, in_specs=in_specs, out_specs=out_specs, out_shape=out_shape,
        scratch_shapes=scratch, compiler_params=_cparams(3),
        name="s5_scan_seg4" if seg4 else "s5_scan",
    )(*args)


def _s5_glu_kernel(x_ref, g_ref, ymix_ref, d_ref, w_ref, b_ref, o_ref):
    x = x_ref[...]
    h = _rmsnorm(x, g_ref[...])
    y = _gelu(ymix_ref[...] + d_ref[...] * h)
    z = _dot(y.astype(BF16), w_ref[...]) + b_ref[...]
    o_ref[...] = x + y * jax.nn.sigmoid(z)


def _s5_glu(x, g, ymix, d_skip, w_glu, b_glu, *, tm):
    n = x.shape[0]
    tile = pl.BlockSpec((tm, D_MODEL), lambda i: (i, 0))
    vec = pl.BlockSpec((1, D_MODEL), lambda i: (0, 0))
    return pl.pallas_call(
        _s5_glu_kernel, grid=(n // tm,),
        in_specs=[tile, vec, tile, vec, _resident((D_MODEL, D_MODEL), lambda i: (0, 0)), vec],
        out_specs=tile, out_shape=jax.ShapeDtypeStruct((n, D_MODEL), F32),
        compiler_params=_cparams(1), name="s5_glu",
    )(x, g, ymix, d_skip, w_glu, b_glu)


def _matmul_kernel(*refs, norm, residual):
    refs = list(refs)
    x_ref = refs.pop(0)
    g_ref = refs.pop(0) if norm else None
    w_ref = refs.pop(0)
    r_ref = refs.pop(0) if residual else None
    o_ref = refs.pop(0)
    x = x_ref[...]
    if norm:
        x = _rmsnorm(x, g_ref[...]).astype(BF16)
    acc = _dot(x, w_ref[...])
    if residual:
        acc = acc + r_ref[...]
    o_ref[...] = acc


def _matmul(x, w, *, tm, tn, g=None, res=None, name):
    n, k = x.shape
    n_out = w.shape[1]
    grid = (n_out // tn, n // tm)
    in_specs = [pl.BlockSpec((tm, k), lambda j, i: (i, 0))]
    args = [x]
    if g is not None:
        in_specs.append(pl.BlockSpec((1, k), lambda j, i: (0, 0)))
        args.append(g)
    in_specs.append(pl.BlockSpec((k, tn), lambda j, i: (0, j)))
    args.append(w)
    if res is not None:
        in_specs.append(pl.BlockSpec((tm, tn), lambda j, i: (i, j)))
        args.append(res)
    return pl.pallas_call(
        functools.partial(_matmul_kernel, norm=g is not None, residual=res is not None),
        grid=grid, in_specs=in_specs,
        out_specs=pl.BlockSpec((tm, tn), lambda j, i: (i, j)),
        out_shape=jax.ShapeDtypeStruct((n, n_out), F32),
        compiler_params=_cparams(2), name=name,
    )(*args)


def _lru_kernel(*refs, seg4):
    if seg4:
        (z_ref, cw_ref, cb_ref, wa_ref, ba_ref, wi_ref, bi_ref, lam_ref, pcv_ref, h0_ref,
         gated_ref, hst_ref, xbst_ref, a_scr, b_scr) = refs
    else:
        (z_ref, cw_ref, cb_ref, wa_ref, ba_ref, wi_ref, bi_ref, lam_ref,
         gated_ref, hst_ref, xbst_ref, a_scr, b_scr, prev_scr, carry_scr) = refs
    tc = pl.program_id(1)
    rows = z_ref.shape[0]
    d = D_MODEL
    gate = z_ref[:, 0:d]
    xb = z_ref[:, d:2 * d]

    row8 = lax.broadcasted_iota(jnp.int32, (SUBLANES, d), 0)
    if seg4:
        t_full = lax.broadcasted_iota(jnp.int32, (rows, d), 0) % CONV_WIDTH
        xc = cb_ref[...] + cw_ref[3:4, :] * xb
        for s in range(1, CONV_WIDTH):
            shifted = jnp.where(t_full >= s, pltpu.roll(xb, s, 0), pcv_ref[s - 1])
            xc = xc + cw_ref[3 - s:4 - s, :] * shifted
        xbst_ref[...] = xb
        rseg = row8 % CONV_WIDTH
    else:
        @pl.when(tc == 0)
        def _():
            prev_scr[...] = jnp.zeros_like(prev_scr)
            carry_scr[...] = jnp.zeros_like(carry_scr)
        xcat = jnp.concatenate([prev_scr[...], xb], axis=0)
        xc = cb_ref[...] + cw_ref[3:4, :] * xb
        for s in range(1, CONV_WIDTH):
            xc = xc + cw_ref[3 - s:4 - s, :] * pltpu.roll(xcat, s, 0)[SUBLANES:]
        tail = xb[rows - SUBLANES:]
        prev_scr[...] = tail
        xbst_ref[0] = tail
        rseg = row8

    r_parts, i_parts = [], []
    for hh in range(LRU_HEADS):
        xh = xc[:, hh * LRU_BLOCK:(hh + 1) * LRU_BLOCK].astype(BF16)
        r_parts.append(_dot(xh, wa_ref[hh]))
        i_parts.append(_dot(xh, wi_ref[hh]))
    r = jax.nn.sigmoid(jnp.concatenate(r_parts, axis=1) + ba_ref[...])
    ig = jax.nn.sigmoid(jnp.concatenate(i_parts, axis=1) + bi_ref[...])
    log_a = -LRU_C * r * jax.nn.softplus(-lam_ref[...])
    a = jnp.exp(log_a)
    a_scr[...] = a
    b_scr[...] = jnp.sqrt(-jnp.tanh(log_a) * (a * a + 1.0)) * (ig * xc)

    masks = [rseg >= s for s in (1, 2, 4)]

    def body(g, carry):
        row = pl.multiple_of(g * SUBLANES, SUBLANES)
        av = a_scr[pl.ds(row, SUBLANES), :]
        bv = b_scr[pl.ds(row, SUBLANES), :]
        for m, s in zip(masks, (1, 2, 4)):
            if seg4 and s == 4:
                continue
            a_sh = jnp.where(m, pltpu.roll(av, s, 0), 1.0)
            b_sh = jnp.where(m, pltpu.roll(bv, s, 0), 0.0)
            bv = bv + av * b_sh
            av = av * a_sh
        c = h0_ref[pl.ds(row, SUBLANES), :] if seg4 else carry
        hv = bv + av * c
        b_scr[pl.ds(row, SUBLANES), :] = hv
        if seg4:
            return carry
        return jnp.broadcast_to(hv[SUBLANES - 1:SUBLANES], (SUBLANES, d))

    if seg4:
        lax.fori_loop(0, rows // SUBLANES, body, 0)
        hst_ref[...] = b_scr[...]
    else:
        c = lax.fori_loop(0, rows // SUBLANES, body, carry_scr[...])
        carry_scr[...] = c
        hst_ref[0] = c
    gated_ref[...] = (b_scr[...] * _gelu(gate)).astype(BF16)


def _lru(z, conv_w, conv_b, w_a, b_a, w_i, b_i, lam, *, n_seq, rows, pcv=None, h0=None):
    n = z.shape[0]
    seg4 = h0 is not None
    d = D_MODEL
    n_chunk = n // (n_seq * rows)

    def row_map(s, t):
        return (s * n_chunk + t, 0)

    vec = pl.BlockSpec((1, d), lambda s, t: (0, 0))
    gate_w = pl.BlockSpec((LRU_HEADS, LRU_BLOCK, LRU_BLOCK), lambda s, t: (0, 0, 0))
    in_specs = [pl.BlockSpec((rows, 2 * d), row_map),
                pl.BlockSpec((CONV_WIDTH, d), lambda s, t: (0, 0)), vec,
                gate_w, vec, gate_w, vec, vec]
    args = [z, conv_w, conv_b, w_a, b_a, w_i, b_i, lam]
    scratch = [pltpu.VMEM((rows, d), F32), pltpu.VMEM((rows, d), F32)]
    tile = pl.BlockSpec((rows, d), row_map)
    if seg4:
        in_specs += [pl.BlockSpec((CONV_WIDTH - 1, rows, d), lambda s, t: (0, s * n_chunk + t, 0)), tile]
        args += [pcv, h0]
        st_shape = jax.ShapeDtypeStruct((n, d), F32)
        st_spec = tile
    else:
        scratch += [pltpu.VMEM((SUBLANES, d), F32), pltpu.VMEM((SUBLANES, d), F32)]
        st_shape = jax.ShapeDtypeStruct((n_seq, SUBLANES, d), F32)
        st_spec = pl.BlockSpec((1, SUBLANES, d), lambda s, t: (s, 0, 0))
    return pl.pallas_call(
        functools.partial(_lru_kernel, seg4=seg4),
        grid=(n_seq, n_chunk), in_specs=in_specs,
        out_specs=(tile, st_spec, st_spec),
        out_shape=(jax.ShapeDtypeStruct((n, d), BF16), st_shape, st_shape),
        scratch_shapes=scratch, compiler_params=_cparams(2),
        name="lru_seg4" if seg4 else "lru",
    )(*args)


NO_RANK = 31.0


def _top_rows(vals_scr, top_scr, n_rows, rank_scr=None):
    if rank_scr is not None:
        rank_scr[...] = jnp.full(rank_scr.shape, NO_RANK, F32)

    def body(r, carry):
        v = vals_scr[0:n_rows]
        m = jnp.max(v, axis=0, keepdims=True)
        top_scr[pl.ds(r, 1), :] = m
        hit = v == m
        if rank_scr is not None:
            rank_scr[...] = jnp.where(hit, r.astype(F32), rank_scr[...])
        vals_scr[0:n_rows] = jnp.where(hit, NEG_INF, v)
        return carry
    lax.fori_loop(0, TOPK + 1, body, 0)


def _peer_retrieve_kernel(y_ref, g_ref, wq_ref, keys_ref,
                          hbt_ref, cnt_ref, rr_ref, rk2_ref, p2_ref,
                          vals_scr, ta_scr, tb_scr, cand_scr, top_scr, rank_scr):
    tm = y_ref.shape[0]
    h = _rmsnorm(y_ref[...], g_ref[...])
    hbt_ref[...] = jnp.transpose(h).astype(BF16)
    hb = h.astype(BF16)
    qb = _dot(hb, wq_ref[...]).astype(BF16)
    row8 = lax.broadcasted_iota(jnp.int32, (SUBLANES, tm), 0)
    for hh in range(PEER_HEADS):
        c0 = hh * 2 * N_KEYS
        s1 = _dot_nt(keys_ref[hh, 0], qb[:, c0:c0 + N_KEYS])
        s2 = _dot_nt(keys_ref[hh, 1], qb[:, c0 + N_KEYS:c0 + 2 * N_KEYS])
        vals_scr[...] = s1
        _top_rows(vals_scr, ta_scr, N_KEYS)
        vals_scr[...] = s2
        _top_rows(vals_scr, tb_scr, N_KEYS, rank_scr)
        a = ta_scr[0:TOPK]
        b = tb_scr[0:TOPK]
        a16 = ta_scr[TOPK:TOPK + 1]
        b16 = tb_scr[TOPK:TOPK + 1]
        a0, b0 = a[0:1], b[0:1]
        a_lo, b_lo = a[0:SUBLANES], b[0:SUBLANES]
        cand_scr[0:16] = a0 + b
        cand_scr[16:24] = a[1:2] + b_lo
        cand_scr[24:32] = jnp.where(row8 < 5, a[2:3] + b_lo, NEG_INF)
        cand_scr[32:40] = jnp.where(row8 < 4, a[3:4] + b_lo, NEG_INF)
        cand_scr[40:48] = jnp.where(row8 < 3, a[4:5] + b_lo, NEG_INF)
        cand_scr[48:56] = a[SUBLANES:2 * SUBLANES] + b0
        rank17 = jnp.where(row8 == 0, a16 + b0, jnp.where(row8 == 1, a0 + b16, NEG_INF))
        cand_scr[56:64] = jnp.where(row8 >= 5, a_lo + b0, rank17)
        cand_scr[64:72] = jnp.where(row8 >= 5, a_lo + b[1:2], NEG_INF)
        cand = cand_scr[...]
        vals_scr[0:72] = cand
        _top_rows(vals_scr, top_scr, 72)
        tau = 0.5 * (top_scr[TOPK - 1:TOPK] + top_scr[TOPK:TOPK + 1])
        z = jnp.sum(jnp.where(cand >= tau, jnp.exp(cand - (a0 + b0)), 0.0), axis=0, keepdims=True)
        cnt = jnp.zeros_like(s1)
        for r in range(TOPK):
            cnt = cnt + jnp.where(s1 + b[r:r + 1] >= tau, 1.0, 0.0)
        cnt_ref[hh] = cnt
        rr_ref[hh] = jnp.exp(s1 - a0 - jnp.log(z))
        rk2_ref[hh] = rank_scr[...].astype(BF16)
        p2_ref[hh] = jnp.exp(s2 - b0).astype(BF16)


def _peer_retrieve(y, g, w_q, keys, *, tm):
    n = y.shape[0]
    k1_shape = jax.ShapeDtypeStruct((PEER_HEADS, N_KEYS, n), F32)
    k2_shape = jax.ShapeDtypeStruct((PEER_HEADS, N_KEYS, n), BF16)
    fac_spec = pl.BlockSpec((PEER_HEADS, N_KEYS, tm), lambda i: (0, 0, i))
    return pl.pallas_call(
        _peer_retrieve_kernel, grid=(n // tm,),
        in_specs=[pl.BlockSpec((tm, D_MODEL), lambda i: (i, 0)),
                  pl.BlockSpec((1, D_MODEL), lambda i: (0, 0)),
                  _resident((D_MODEL, D_MODEL), lambda i: (0, 0)),
                  pl.BlockSpec((PEER_HEADS, 2, N_KEYS, N_KEYS), lambda i: (0, 0, 0, 0))],
        out_specs=(pl.BlockSpec((D_MODEL, tm), lambda i: (0, i)),
                   fac_spec, fac_spec, fac_spec, fac_spec),
        out_shape=(jax.ShapeDtypeStruct((D_MODEL, n), BF16),
                   k1_shape, k1_shape, k2_shape, k2_shape),
        scratch_shapes=[pltpu.VMEM((N_KEYS, tm), F32), pltpu.VMEM((24, tm), F32),
                        pltpu.VMEM((24, tm), F32), pltpu.VMEM((72, tm), F32),
                        pltpu.VMEM((24, tm), F32), pltpu.VMEM((N_KEYS, tm), F32)],
        compiler_params=_cparams(1), name="peer_retrieve",
    )(y, g, w_q, keys)


def _peer_dense_kernel(*refs, eb, final_norm):
    if final_norm:
        (hbt_ref, y_ref, u_ref, vlo_ref, vhi_ref, cnt_ref, rr_ref, cntn_ref, rrn_ref, rk2_ref, p2_ref,
         gf_ref, o_ref, wa_scr, wb_scr, ga_scr, gb_scr, bc_scr) = refs
    else:
        (hbt_ref, y_ref, u_ref, vlo_ref, vhi_ref, cnt_ref, rr_ref, cntn_ref, rrn_ref, rk2_ref, p2_ref,
         o_ref, wa_scr, wb_scr, ga_scr, gb_scr, bc_scr) = refs
    j = pl.program_id(1)
    last = pl.num_programs(1) - 1
    n_sub = eb // N_KEYS
    tm = o_ref.shape[0]

    def gates(cnt_blk, rr_blk, half, dst_scr):
        for hh in range(PEER_HEADS):
            for i in range(n_sub):
                r = half * n_sub + i
                k = (hh * n_sub + i) * BF16_ROWS
                bc_scr[0, k:k + BF16_ROWS, :] = jnp.broadcast_to(
                    cnt_blk[hh, 0, r:r + 1, :].astype(BF16), (BF16_ROWS, tm))
                bc_scr[1, k:k + BF16_ROWS, :] = jnp.broadcast_to(
                    rr_blk[hh, 0, r:r + 1, :].astype(BF16), (BF16_ROWS, tm))
        for i in range(n_sub):
            for c in range(N_KEYS // BF16_ROWS):
                rows = slice(c * BF16_ROWS, (c + 1) * BF16_ROWS)
                gate = jnp.zeros((BF16_ROWS, tm), BF16)
                for hh in range(PEER_HEADS):
                    k = (hh * n_sub + i) * BF16_ROWS
                    cnt = bc_scr[0, k:k + BF16_ROWS, :]
                    rr = bc_scr[1, k:k + BF16_ROWS, :]
                    gate = gate + jnp.where(rk2_ref[hh, rows, :] < cnt, p2_ref[hh, rows, :] * rr,
                                            jnp.zeros((), BF16))
                dst_scr[i * N_KEYS + c * BF16_ROWS:i * N_KEYS + (c + 1) * BF16_ROWS, :] = gate

    def up(half, g_scr, dst_scr):
        act = _gelu(_dot(u_ref[half * eb:(half + 1) * eb, :], hbt_ref[...]))
        dst_scr[...] = act.astype(BF16) * g_scr[...]

    @pl.when(j == 0)
    def _():
        o_ref[...] = y_ref[...]
        wb_scr[...] = jnp.zeros_like(wb_scr)
        gates(cnt_ref, rr_ref, 0, ga_scr)

    o_ref[...] += _dot_tn(wb_scr[...], vlo_ref[...])
    up(0, ga_scr, wa_scr)
    gates(cnt_ref, rr_ref, 1, gb_scr)

    @pl.when(j < last)
    def _():
        o_ref[...] += _dot_tn(wa_scr[...], vhi_ref[...])
        up(1, gb_scr, wb_scr)
        gates(cntn_ref, rrn_ref, 0, ga_scr)

    if final_norm:
        @pl.when(j == last)
        def _():
            o_ref[...] = _rmsnorm(o_ref[...], gf_ref[...])


def _peer_dense(hbt, y, u, v, cnt, rr, rk2, p2, *, tm, eb, g_final=None):
    n = y.shape[0]
    n_blk = N_EXPERTS // eb
    n_pair = n_blk // 2
    n_sub = eb // N_KEYS
    cnt = cnt.reshape(PEER_HEADS, n_pair, 2 * n_sub, n)
    rr = rr.reshape(PEER_HEADS, n_pair, 2 * n_sub, n)
    tile = pl.BlockSpec((tm, D_MODEL), lambda i, j: (i, 0))
    fac_blk = pl.BlockSpec((PEER_HEADS, 1, 2 * n_sub, tm),
                           lambda i, j: (0, jnp.minimum(j, n_pair - 1), 0, i))
    fac_next = pl.BlockSpec((PEER_HEADS, 1, 2 * n_sub, tm),
                            lambda i, j: (0, jnp.minimum(j + 1, n_pair - 1), 0, i))
    in_specs = [
        _resident((D_MODEL, tm), lambda i, j: (0, i)),
        _resident((tm, D_MODEL), lambda i, j: (i, 0)),
        pl.BlockSpec((2 * eb, D_MODEL), lambda i, j: (jnp.minimum(j, n_pair - 1), 0)),
        pl.BlockSpec((eb, D_MODEL), lambda i, j: (jnp.maximum(2 * j - 1, 0), 0)),
        pl.BlockSpec((eb, D_MODEL), lambda i, j: (jnp.minimum(2 * j, n_blk - 1), 0)),
        fac_blk, fac_blk, fac_next, fac_next,
        _resident((PEER_HEADS, N_KEYS, tm), lambda i, j: (0, 0, i)),
        _resident((PEER_HEADS, N_KEYS, tm), lambda i, j: (0, 0, i)),
    ]
    args = [hbt, y, u, v, v, cnt, rr, cnt, rr, rk2, p2]
    if g_final is not None:
        in_specs.append(pl.BlockSpec((1, D_MODEL), lambda i, j: (0, 0)))
        args.append(g_final)
    return pl.pallas_call(
        functools.partial(_peer_dense_kernel, eb=eb, final_norm=g_final is not None),
        grid=(n // tm, n_pair + 1), in_specs=in_specs, out_specs=tile,
        out_shape=jax.ShapeDtypeStruct((n, D_MODEL), F32),
        scratch_shapes=[pltpu.VMEM((eb, tm), BF16)] * 4
        + [pltpu.VMEM((2, PEER_HEADS * n_sub * BF16_ROWS, tm), BF16)],
        compiler_params=_cparams(2), name="peer_dense",
    )(*args)


def _peer(y, g, w_q, keys, u, v, *, tm_retrieve, tm, eb, g_final=None):
    hbt, cnt, rr, rk2, p2 = _peer_retrieve(y, g, w_q, keys, tm=tm_retrieve)
    return _peer_dense(hbt, y, u, v, cnt, rr, rk2, p2, tm=tm, eb=eb, g_final=g_final)


PEER_TM = 512
PEER_DENSE_TM = 512
PEER_EB = 512
SEQ_ROWS = 256
S5_ROWS = 512
CAST_ROWS = 1024


def _row(v):
    return v.reshape(1, -1)


@jax.jit
def _step(x_prompt, x_sample, state_s5_re, state_s5_im, state_lru_h, state_lru_conv,
          norm_mix, norm_ffn, norm_final,
          s5_lam_re, s5_lam_im, s5_log_dt, s5_b_re, s5_b_im, s5_c_re, s5_c_im, s5_d, s5_w_glu, s5_b_glu,
          lru_w_in, lru_conv_w, lru_conv_b, lru_w_a, lru_b_a, lru_w_i, lru_b_i, lru_lam, lru_w_out,
          peer_w_q, peer_sub_keys, peer_u, peer_v):
    bsz, seq, d = x_prompt.shape
    dec_b, dec_t, _ = x_sample.shape
    xp = x_prompt.reshape(bsz * seq, d)
    xs = x_sample.reshape(dec_b * dec_t, d)
    n_s = dec_b * dec_t

    pow_re, pow_im, bb_re, bb_im = _s5_discretize(
        s5_lam_re[0], s5_lam_im[0], s5_log_dt[0], s5_b_re[0], s5_b_im[0])
    g_mix0 = _row(norm_mix[0])
    w_glu = s5_w_glu[0].astype(BF16)
    outs = []
    for mode, x in (("prompt", xp), ("sample", xs)):
        if mode == "prompt":
            wb, wc, cre, cim = _s5_weights(pow_re, pow_im, bb_re, bb_im, s5_c_re[0], s5_c_im[0], None)
            ymix, sre, sim = _s5_scan(x, g_mix0, wb, wc, cre, cim, n_seq=bsz, rows=S5_ROWS)
            st = (sre[:, :, 0, :].reshape(1, bsz, S5_GROUPS, S5_STATE),
                  sim[:, :, 0, :].reshape(1, bsz, S5_GROUPS, S5_STATE))
        else:
            wb, wc, cre, cim = _s5_weights(pow_re, pow_im, bb_re, bb_im, s5_c_re[0], s5_c_im[0], dec_t)

            def rep(s):
                s = jnp.transpose(s.reshape(dec_b, S5_KB, S5_SW), (1, 0, 2))
                return jnp.repeat(s, dec_t, axis=1)
            ymix, sre, sim = _s5_scan(x, g_mix0, wb, wc, cre, cim, n_seq=n_s // S5_ROWS, rows=S5_ROWS,
                                      h0=(rep(state_s5_re[0]), rep(state_s5_im[0])))

            def last(s):
                s = s.reshape(S5_KB, dec_b, dec_t, S5_SW)[:, :, dec_t - 1, :]
                return jnp.transpose(s, (1, 0, 2)).reshape(1, dec_b, S5_GROUPS, S5_STATE)
            st = (last(sre), last(sim))
        y = _s5_glu(x, g_mix0, ymix, _row(s5_d[0]), w_glu, _row(s5_b_glu[0]), tm=PEER_TM)
        outs.append((y, st))
    (yp, s5_p), (ys, s5_s) = outs

    tables = [(_cast_bf16(peer_u[i], rows=CAST_ROWS), _cast_bf16(peer_v[i], rows=CAST_ROWS))
              for i in range(peer_u.shape[0])]

    def peer_layer(i, y, g_final=None):
        return _peer(y, _row(norm_ffn[i]), peer_w_q[i].astype(BF16), peer_sub_keys[i].astype(BF16),
                     tables[i][0], tables[i][1],
                     tm_retrieve=PEER_TM, tm=min(PEER_DENSE_TM, y.shape[0]), eb=PEER_EB, g_final=g_final)
    yp = peer_layer(0, yp)
    ys = peer_layer(0, ys)

    g_mix1 = _row(norm_mix[1])
    w_in = lru_w_in[0].astype(BF16)
    w_out = lru_w_out[0].astype(BF16)
    gate_args = (lru_conv_w[0], _row(lru_conv_b[0]), lru_w_a[0].astype(BF16), _row(lru_b_a[0]),
                 lru_w_i[0].astype(BF16), _row(lru_b_i[0]), _row(lru_lam[0]))

    zp = _matmul(yp, w_in, tm=PEER_TM, tn=D_MODEL, g=g_mix1, name="lru_in_proj")
    gated_p, hst_p, xbst_p = _lru(zp, *gate_args, n_seq=bsz, rows=SEQ_ROWS)
    yp = _matmul(gated_p, w_out, tm=PEER_TM, tn=D_MODEL, res=yp, name="lru_out_proj")
    lru_h_p = hst_p[:, 0, :].reshape(1, bsz, d)
    lru_c_p = xbst_p[:, SUBLANES - (CONV_WIDTH - 1):, :].reshape(1, bsz, CONV_WIDTH - 1, d)

    buf = state_lru_conv[0]
    zero = jnp.zeros((dec_b, 1, d), F32)
    pcv = jnp.stack([
        jnp.concatenate([buf[:, 2:3], zero, zero, zero], axis=1),
        jnp.concatenate([buf[:, 1:3], zero, zero], axis=1),
        jnp.concatenate([buf[:, 0:3], zero], axis=1),
    ]).reshape(CONV_WIDTH - 1, n_s, d)
    h0 = jnp.repeat(state_lru_h[0], dec_t, axis=0)
    zs = _matmul(ys, w_in, tm=PEER_TM, tn=D_MODEL, g=g_mix1, name="lru_in_proj")
    gated_s, hst_s, xbst_s = _lru(zs, *gate_args, n_seq=n_s // SEQ_ROWS, rows=SEQ_ROWS, pcv=pcv, h0=h0)
    ys = _matmul(gated_s, w_out, tm=PEER_TM, tn=D_MODEL, res=ys, name="lru_out_proj")
    lru_h_s = hst_s.reshape(dec_b, dec_t, d)[:, dec_t - 1].reshape(1, dec_b, d)
    lru_c_s = xbst_s.reshape(dec_b, dec_t, d)[:, 1:].reshape(1, dec_b, CONV_WIDTH - 1, d)

    g_fin = _row(norm_final)
    yp = peer_layer(1, yp, g_final=g_fin)
    ys = peer_layer(1, ys, g_final=g_fin)

    return (yp.reshape(bsz, seq, d), ys.reshape(dec_b, dec_t, d),
            s5_p[0], s5_p[1], s5_s[0], s5_s[1],
            lru_h_p, lru_c_p, lru_h_s, lru_c_s)


def kernel(x_prompt, x_sample, state_s5_re, state_s5_im, state_lru_h, state_lru_conv, norm_mix, norm_ffn, norm_final, s5_lam_re, s5_lam_im, s5_log_dt, s5_b_re, s5_b_im, s5_c_re, s5_c_im, s5_d, s5_w_glu, s5_b_glu, lru_w_in, lru_conv_w, lru_conv_b, lru_w_a, lru_b_a, lru_w_i, lru_b_i, lru_lam, lru_w_out, peer_w_q, peer_sub_keys, peer_u, peer_v):
    return _step(x_prompt, x_sample, state_s5_re, state_s5_im, state_lru_h, state_lru_conv,
                 norm_mix, norm_ffn, norm_final,
                 s5_lam_re, s5_lam_im, s5_log_dt, s5_b_re, s5_b_im, s5_c_re, s5_c_im, s5_d, s5_w_glu, s5_b_glu,
                 lru_w_in, lru_conv_w, lru_conv_b, lru_w_a, lru_b_a, lru_w_i, lru_b_i, lru_lam, lru_w_out,
                 peer_w_q, peer_sub_keys, peer_u, peer_v)
```

```python
import functools
import math

import jax
import jax.numpy as jnp
from jax import lax
from jax.experimental import pallas as pl
from jax.experimental.pallas import tpu as pltpu

F32 = jnp.float32
BF16 = jnp.bfloat16

D_MODEL = 2048
RMS_EPS = 1e-6
S5_GROUP = 16
S5_GROUPS = 128
S5_STATE = 64
S5_KB = 8
S5_KBW = 256
S5_SW = 1024
LRU_HEADS = 8
LRU_BLOCK = 256
CONV_WIDTH = 4
LRU_C = 8.0
PEER_HEADS = 8
N_KEYS = 128
N_EXPERTS = N_KEYS * N_KEYS
TOPK = 16
SUBLANES = 8
LANES = 128
BF16_ROWS = 2 * SUBLANES
VMEM_LIMIT_BYTES = 56 * 1024 * 1024

NEG_INF = float("-inf")


def _cparams(n_axes, flags=None):
    return pltpu.CompilerParams(
        dimension_semantics=("arbitrary",) * n_axes,
        vmem_limit_bytes=VMEM_LIMIT_BYTES,
        flags=flags,
    )


def _resident(block_shape, index_map):
    return pl.BlockSpec(block_shape, index_map, pipeline_mode=pl.Buffered(1))


def _rmsnorm(x, g):
    ms = jnp.mean(x * x, axis=-1, keepdims=True)
    return x * lax.rsqrt(ms + RMS_EPS) * g


def _gelu(x):
    return 0.5 * x * (1.0 + lax.erf(x * (1.0 / math.sqrt(2.0))))


def _dot(a, b):
    return jnp.dot(a, b, preferred_element_type=F32)


def _dot_nt(a, b):
    return lax.dot_general(a, b, (((1,), (1,)), ((), ())), preferred_element_type=F32)


def _dot_tn(a, b):
    return lax.dot_general(a, b, (((0,), (0,)), ((), ())), preferred_element_type=F32)


def _cast_kernel(x_ref, o_ref):
    o_ref[...] = x_ref[...].astype(BF16)


def _cast_bf16(x, *, rows):
    layers, n, d = x.shape
    spec = pl.BlockSpec((None, rows, d), lambda l, i: (l, i, 0))
    return pl.pallas_call(
        _cast_kernel, grid=(layers, n // rows), in_specs=[spec], out_specs=spec,
        out_shape=jax.ShapeDtypeStruct((layers, n, d), BF16),
        compiler_params=_cparams(2), name="cast_bf16",
    )(x)


def _s5_discretize_kernel(lre_ref, lim_ref, ldt_ref, bre_ref, bim_ref,
                          pre_ref, pim_ref, bbre_ref, bbim_ref):
    lr = lre_ref[...]
    li = lim_ref[...]
    dt = jnp.exp(ldt_ref[...])
    mag = jnp.exp(lr * dt)
    ab_re = mag * jnp.cos(li * dt)
    ab_im = mag * jnp.sin(li * dt)
    nr, ni = ab_re - 1.0, ab_im
    den = lr * lr + li * li
    f_re = (nr * lr + ni * li) / den
    f_im = (ni * lr - nr * li) / den
    for c in range(S5_GROUP):
        br = bre_ref[c]
        bi = bim_ref[c]
        bbre_ref[c] = f_re * br - f_im * bi
        bbim_ref[c] = f_re * bi + f_im * br
    p_re, p_im = ab_re, ab_im
    for k in range(SUBLANES):
        pre_ref[k] = p_re
        pim_ref[k] = p_im
        p_re, p_im = p_re * ab_re - p_im * ab_im, p_re * ab_im + p_im * ab_re


def _s5_discretize(lam_re, lam_im, log_dt, b_re, b_im):
    g, p = S5_GROUPS, S5_STATE
    b_re_t = jnp.transpose(b_re, (2, 0, 1))
    b_im_t = jnp.transpose(b_im, (2, 0, 1))
    out_shape = (
        jax.ShapeDtypeStruct((SUBLANES, g, p), F32),
        jax.ShapeDtypeStruct((SUBLANES, g, p), F32),
        jax.ShapeDtypeStruct((S5_GROUP, g, p), F32),
        jax.ShapeDtypeStruct((S5_GROUP, g, p), F32),
    )
    return pl.pallas_call(_s5_discretize_kernel, out_shape=out_shape, name="s5_discretize")(
        lam_re, lam_im, log_dt.reshape(g, 1), b_re_t, b_im_t)


def _s5_proj_weights(bb_re, bb_im, c_re, c_im):
    eye = jnp.eye(S5_GROUP, dtype=F32)

    def in_proj(bb):
        bb = bb.reshape(S5_GROUP, S5_KB, S5_GROUP, S5_STATE)
        w = jnp.einsum("ckgp,gh->kgchp", bb, eye)
        return w.reshape(S5_KB, S5_KBW, S5_SW)

    def out_proj(c):
        c = c.reshape(S5_KB, S5_GROUP, S5_GROUP, S5_STATE)
        w = jnp.einsum("kgcp,gh->kgphc", c, eye)
        return w.reshape(S5_KB, S5_SW, S5_KBW)

    wb = jnp.concatenate([in_proj(bb_re), in_proj(bb_im)], axis=2).astype(BF16)
    wc = jnp.concatenate([out_proj(c_re), out_proj(-c_im)], axis=1).astype(BF16)
    return wb, wc


def _s5_short_consts(pw, seg):
    pos = jnp.arange(SUBLANES) % seg
    pw = pw.reshape(SUBLANES, S5_KB, S5_SW)
    steps = [pw[d - 1][:, None, :] * (pos >= d).astype(F32)[None, :, None] for d in (1, 2)]
    init = jnp.transpose(pw[pos], (1, 0, 2))
    return jnp.stack(steps + [init], axis=1)


def _s5_scan_short_kernel(x_ref, g_ref, wb_ref, wc_ref, cre_ref, cim_ref, h0re_ref, h0im_ref,
                          ymix_ref, sre_ref, sim_ref, hb_scr, bu_scr):
    kb = pl.program_id(1)
    rows = x_ref.shape[0]

    @pl.when(kb == 0)
    def _():
        hb = _rmsnorm(x_ref[...], g_ref[...]).astype(BF16)
        for j in range(S5_KB):
            hb_scr[j] = hb[:, j * S5_KBW:(j + 1) * S5_KBW]

    bu_scr[...] = _dot(hb_scr[kb], wb_ref[0])

    def body(r, carry):
        row = pl.multiple_of(r * SUBLANES, SUBLANES)
        re = bu_scr[pl.ds(row, SUBLANES), 0:S5_SW]
        im = bu_scr[pl.ds(row, SUBLANES), S5_SW:2 * S5_SW]
        for idx, d in enumerate((1, 2)):
            ar = cre_ref[0, idx]
            ai = cim_ref[0, idx]
            sr = pltpu.roll(re, d, 0)
            si = pltpu.roll(im, d, 0)
            re, im = re + ar * sr - ai * si, im + ar * si + ai * sr
        pr = cre_ref[0, 2]
        pi = cim_ref[0, 2]
        cr = h0re_ref[0, pl.ds(row, SUBLANES), :]
        ci = h0im_ref[0, pl.ds(row, SUBLANES), :]
        bu_scr[pl.ds(row, SUBLANES), 0:S5_SW] = re + pr * cr - pi * ci
        bu_scr[pl.ds(row, SUBLANES), S5_SW:2 * S5_SW] = im + pr * ci + pi * cr
        return carry

    lax.fori_loop(0, rows // SUBLANES, body, 0)
    sre_ref[0] = bu_scr[:, 0:S5_SW]
    sim_ref[0] = bu_scr[:, S5_SW:2 * S5_SW]
    ymix_ref[...] = _dot(bu_scr[...].astype(BF16), wc_ref[0])


def _s5_scan_short(x, g, wb, wc, cre, cim, h0, *, rows):
    n = x.shape[0]
    c_spec = pl.BlockSpec((1, 3, SUBLANES, S5_SW), lambda i, k: (k, 0, 0, 0))
    st_spec = pl.BlockSpec((1, rows, S5_SW), lambda i, k: (k, i, 0))
    st_shape = jax.ShapeDtypeStruct((S5_KB, n, S5_SW), F32)
    return pl.pallas_call(
        _s5_scan_short_kernel, grid=(n // rows, S5_KB),
        in_specs=[pl.BlockSpec((rows, D_MODEL), lambda i, k: (i, 0)),
                  pl.BlockSpec((1, D_MODEL), lambda i, k: (0, 0)),
                  pl.BlockSpec((1, S5_KBW, 2 * S5_SW), lambda i, k: (k, 0, 0)),
                  pl.BlockSpec((1, 2 * S5_SW, S5_KBW), lambda i, k: (k, 0, 0)),
                  c_spec, c_spec, st_spec, st_spec],
        out_specs=(pl.BlockSpec((rows, S5_KBW), lambda i, k: (i, k)), st_spec, st_spec),
        out_shape=(jax.ShapeDtypeStruct((n, D_MODEL), F32), st_shape, st_shape),
        scratch_shapes=[pltpu.VMEM((S5_KB, rows, S5_KBW), BF16), pltpu.VMEM((rows, 2 * S5_SW), F32)],
        compiler_params=_cparams(2), name="s5_scan_short",
    )(x, g, wb, wc, cre, cim, h0[0], h0[1])


S5_CT = 2 * S5_SW // LANES


def _s5_scan_long_kernel(x_ref, g_ref, wb_ref, wc_ref, are_ref, aim_ref,
                         ymix_ref, sre_ref, sim_ref, hb_scr, t_scr, carry_scr):
    tc = pl.program_id(0)
    kb = pl.program_id(1)
    n_seq, rows = x_ref.shape[0], x_ref.shape[1]
    n_grp = rows // SUBLANES
    half = S5_CT // 2

    @pl.when(kb == 0)
    def _():
        for b in range(n_seq):
            hb = _rmsnorm(x_ref[b], g_ref[...]).astype(BF16)
            for j in range(S5_KB):
                hb_scr[j, b * rows:(b + 1) * rows, :] = hb[:, j * S5_KBW:(j + 1) * S5_KBW]

    @pl.when(jnp.logical_and(tc == 0, kb == 0))
    def _():
        carry_scr[...] = jnp.zeros_like(carry_scr)

    bu = _dot(hb_scr[kb], wb_ref[0])
    for b in range(n_seq):
        for g in range(n_grp):
            r0 = b * rows + g * SUBLANES
            for ct in range(S5_CT):
                t0 = (g * S5_CT + ct) * SUBLANES
                t_scr[b, t0:t0 + SUBLANES, :] = bu[r0:r0 + SUBLANES, ct * LANES:(ct + 1) * LANES]

    ar = are_ref[0]
    ai = aim_ref[0]

    def body(g, carry):
        carry = list(carry)
        for s in range(SUBLANES):
            for b in range(n_seq):
                i_re = pl.ds(g * (S5_CT * SUBLANES) + s, SUBLANES, stride=SUBLANES)
                i_im = pl.ds(g * (S5_CT * SUBLANES) + half * SUBLANES + s, SUBLANES, stride=SUBLANES)
                sr, si = carry[2 * b], carry[2 * b + 1]
                sr, si = (ar * sr - ai * si + t_scr[b, i_re, :],
                          ar * si + ai * sr + t_scr[b, i_im, :])
                t_scr[b, i_re, :] = sr
                t_scr[b, i_im, :] = si
                carry[2 * b], carry[2 * b + 1] = sr, si
        return tuple(carry)

    c0 = tuple(carry_scr[kb, i] for i in range(2 * n_seq))
    c = lax.fori_loop(0, n_grp, body, c0)
    for b in range(n_seq):
        carry_scr[kb, 2 * b] = c[2 * b]
        carry_scr[kb, 2 * b + 1] = c[2 * b + 1]
        sre_ref[kb, b] = c[2 * b]
        sim_ref[kb, b] = c[2 * b + 1]

    row_blocks = []
    for b in range(n_seq):
        for g in range(n_grp):
            tiles = [t_scr[b, (g * S5_CT + ct) * SUBLANES:(g * S5_CT + ct + 1) * SUBLANES, :]
                     for ct in range(S5_CT)]
            row_blocks.append(jnp.concatenate(tiles, axis=1))
    states = jnp.concatenate(row_blocks, axis=0).astype(BF16)
    ymix = _dot(states, wc_ref[0])
    for b in range(n_seq):
        ymix_ref[b] = ymix[b * rows:(b + 1) * rows]


def _s5_scan_long(x, g, wb, wc, a_re, a_im, *, rows):
    n_seq, t_len, _ = x.shape
    grid = (t_len // rows, S5_KB)
    st_shape = jax.ShapeDtypeStruct((S5_KB, n_seq, SUBLANES, LANES), F32)
    st_spec = pl.BlockSpec((S5_KB, n_seq, SUBLANES, LANES), lambda t, k: (0, 0, 0, 0))
    a_spec = pl.BlockSpec((1, SUBLANES, LANES), lambda t, k: (k, 0, 0))
    return pl.pallas_call(
        _s5_scan_long_kernel, grid=grid,
        in_specs=[pl.BlockSpec((n_seq, rows, D_MODEL), lambda t, k: (0, t, 0)),
                  pl.BlockSpec((1, D_MODEL), lambda t, k: (0, 0)),
                  pl.BlockSpec((1, S5_KBW, 2 * S5_SW), lambda t, k: (k, 0, 0)),
                  pl.BlockSpec((1, 2 * S5_SW, S5_KBW), lambda t, k: (k, 0, 0)),
                  a_spec, a_spec],
        out_specs=(pl.BlockSpec((n_seq, rows, S5_KBW), lambda t, k: (0, t, k)), st_spec, st_spec),
        out_shape=(jax.ShapeDtypeStruct((n_seq, t_len, D_MODEL), F32), st_shape, st_shape),
        scratch_shapes=[pltpu.VMEM((S5_KB, n_seq * rows, S5_KBW), BF16),
                        pltpu.VMEM((n_seq, rows * S5_CT, LANES), F32),
                        pltpu.VMEM((S5_KB, 2 * n_seq, SUBLANES, LANES), F32)],
        compiler_params=_cparams(2), name="s5_scan_long",
    )(x, g, wb, wc, a_re, a_im)


def _s5_glu_kernel(x_ref, g_ref, ymix_ref, d_ref, w_ref, b_ref, o_ref):
    x = x_ref[...]
    h = _rmsnorm(x, g_ref[...])
    y = _gelu(ymix_ref[...] + d_ref[...] * h)
    z = _dot(y.astype(BF16), w_ref[...]) + b_ref[...]
    o_ref[...] = x + y * jax.nn.sigmoid(z)


def _s5_glu(x, g, ymix, d_skip, w_glu, b_glu, *, tm):
    n = x.shape[0]
    tile = pl.BlockSpec((tm, D_MODEL), lambda i: (i, 0))
    vec = pl.BlockSpec((1, D_MODEL), lambda i: (0, 0))
    return pl.pallas_call(
        _s5_glu_kernel, grid=(n // tm,),
        in_specs=[tile, vec, tile, vec, _resident((D_MODEL, D_MODEL), lambda i: (0, 0)), vec],
        out_specs=tile, out_shape=jax.ShapeDtypeStruct((n, D_MODEL), F32),
        compiler_params=_cparams(1), name="s5_glu",
    )(x, g, ymix, d_skip, w_glu, b_glu)


def _matmul_kernel(*refs, norm, residual):
    refs = list(refs)
    x_ref = refs.pop(0)
    g_ref = refs.pop(0) if norm else None
    w_ref = refs.pop(0)
    r_ref = refs.pop(0) if residual else None
    o_ref = refs.pop(0)
    x = x_ref[...]
    if norm:
        x = _rmsnorm(x, g_ref[...]).astype(BF16)
    acc = _dot(x, w_ref[...])
    if residual:
        acc = acc + r_ref[...]
    o_ref[...] = acc


def _matmul(x, w, *, tm, tn, g=None, res=None, name):
    n, k = x.shape
    n_out = w.shape[1]
    grid = (n_out // tn, n // tm)
    in_specs = [pl.BlockSpec((tm, k), lambda j, i: (i, 0))]
    args = [x]
    if g is not None:
        in_specs.append(pl.BlockSpec((1, k), lambda j, i: (0, 0)))
        args.append(g)
    in_specs.append(pl.BlockSpec((k, tn), lambda j, i: (0, j)))
    args.append(w)
    if res is not None:
        in_specs.append(pl.BlockSpec((tm, tn), lambda j, i: (i, j)))
        args.append(res)
    return pl.pallas_call(
        functools.partial(_matmul_kernel, norm=g is not None, residual=res is not None),
        grid=grid, in_specs=in_specs,
        out_specs=pl.BlockSpec((tm, tn), lambda j, i: (i, j)),
        out_shape=jax.ShapeDtypeStruct((n, n_out), F32),
        compiler_params=_cparams(2), name=name,
    )(*args)


def _lru_kernel(*refs, seg4):
    if seg4:
        (z_ref, cw_ref, cb_ref, wa_ref, ba_ref, wi_ref, bi_ref, lam_ref, pcv_ref, h0_ref,
         gated_ref, hst_ref, xbst_ref, a_scr, b_scr) = refs
    else:
        (z_ref, cw_ref, cb_ref, wa_ref, ba_ref, wi_ref, bi_ref, lam_ref,
         gated_ref, hst_ref, xbst_ref, a_scr, b_scr, prev_scr, carry_scr) = refs
    tc = pl.program_id(1)
    rows = z_ref.shape[0]
    d = D_MODEL
    gate = z_ref[:, 0:d]
    xb = z_ref[:, d:2 * d]

    row8 = lax.broadcasted_iota(jnp.int32, (SUBLANES, d), 0)
    if seg4:
        t_full = lax.broadcasted_iota(jnp.int32, (rows, d), 0) % CONV_WIDTH
        xc = cb_ref[...] + cw_ref[3:4, :] * xb
        for s in range(1, CONV_WIDTH):
            shifted = jnp.where(t_full >= s, pltpu.roll(xb, s, 0), pcv_ref[s - 1])
            xc = xc + cw_ref[3 - s:4 - s, :] * shifted
        xbst_ref[...] = xb
        rseg = row8 % CONV_WIDTH
    else:
        @pl.when(tc == 0)
        def _():
            prev_scr[...] = jnp.zeros_like(prev_scr)
            carry_scr[...] = jnp.zeros_like(carry_scr)
        xcat = jnp.concatenate([prev_scr[...], xb], axis=0)
        xc = cb_ref[...] + cw_ref[3:4, :] * xb
        for s in range(1, CONV_WIDTH):
            xc = xc + cw_ref[3 - s:4 - s, :] * pltpu.roll(xcat, s, 0)[SUBLANES:]
        tail = xb[rows - SUBLANES:]
        prev_scr[...] = tail
        xbst_ref[0] = tail
        rseg = row8

    r_parts, i_parts = [], []
    for hh in range(LRU_HEADS):
        xh = xc[:, hh * LRU_BLOCK:(hh + 1) * LRU_BLOCK].astype(BF16)
        r_parts.append(_dot(xh, wa_ref[hh]))
        i_parts.append(_dot(xh, wi_ref[hh]))
    r = jax.nn.sigmoid(jnp.concatenate(r_parts, axis=1) + ba_ref[...])
    ig = jax.nn.sigmoid(jnp.concatenate(i_parts, axis=1) + bi_ref[...])
    log_a = -LRU_C * r * jax.nn.softplus(-lam_ref[...])
    a = jnp.exp(log_a)
    a_scr[...] = a
    b_scr[...] = jnp.sqrt(-jnp.tanh(log_a) * (a * a + 1.0)) * (ig * xc)

    masks = [rseg >= s for s in (1, 2, 4)]

    def body(g, carry):
        row = pl.multiple_of(g * SUBLANES, SUBLANES)
        av = a_scr[pl.ds(row, SUBLANES), :]
        bv = b_scr[pl.ds(row, SUBLANES), :]
        for m, s in zip(masks, (1, 2, 4)):
            if seg4 and s == 4:
                continue
            a_sh = jnp.where(m, pltpu.roll(av, s, 0), 1.0)
            b_sh = jnp.where(m, pltpu.roll(bv, s, 0), 0.0)
            bv = bv + av * b_sh
            av = av * a_sh
        c = h0_ref[pl.ds(row, SUBLANES), :] if seg4 else carry
        hv = bv + av * c
        b_scr[pl.ds(row, SUBLANES), :] = hv
        if seg4:
            return carry
        return jnp.broadcast_to(hv[SUBLANES - 1:SUBLANES], (SUBLANES, d))

    if seg4:
        lax.fori_loop(0, rows // SUBLANES, body, 0)
        hst_ref[...] = b_scr[...]
    else:
        c = lax.fori_loop(0, rows // SUBLANES, body, carry_scr[...])
        carry_scr[...] = c
        hst_ref[0] = c
    gated_ref[...] = (b_scr[...] * _gelu(gate)).astype(BF16)


def _lru(z, conv_w, conv_b, w_a, b_a, w_i, b_i, lam, *, n_seq, rows, pcv=None, h0=None):
    n = z.shape[0]
    seg4 = h0 is not None
    d = D_MODEL
    n_chunk = n // (n_seq * rows)

    def row_map(s, t):
        return (s * n_chunk + t, 0)

    vec = pl.BlockSpec((1, d), lambda s, t: (0, 0))
    gate_w = pl.BlockSpec((LRU_HEADS, LRU_BLOCK, LRU_BLOCK), lambda s, t: (0, 0, 0))
    in_specs = [pl.BlockSpec((rows, 2 * d), row_map),
                pl.BlockSpec((CONV_WIDTH, d), lambda s, t: (0, 0)), vec,
                gate_w, vec, gate_w, vec, vec]
    args = [z, conv_w, conv_b, w_a, b_a, w_i, b_i, lam]
    scratch = [pltpu.VMEM((rows, d), F32), pltpu.VMEM((rows, d), F32)]
    tile = pl.BlockSpec((rows, d), row_map)
    if seg4:
        in_specs += [pl.BlockSpec((CONV_WIDTH - 1, rows, d), lambda s, t: (0, s * n_chunk + t, 0)), tile]
        args += [pcv, h0]
        st_shape = jax.ShapeDtypeStruct((n, d), F32)
        st_spec = tile
    else:
        scratch += [pltpu.VMEM((SUBLANES, d), F32), pltpu.VMEM((SUBLANES, d), F32)]
        st_shape = jax.ShapeDtypeStruct((n_seq, SUBLANES, d), F32)
        st_spec = pl.BlockSpec((1, SUBLANES, d), lambda s, t: (s, 0, 0))
    return pl.pallas_call(
        functools.partial(_lru_kernel, seg4=seg4),
        grid=(n_seq, n_chunk), in_specs=in_specs,
        out_specs=(tile, st_spec, st_spec),
        out_shape=(jax.ShapeDtypeStruct((n, d), BF16), st_shape, st_shape),
        scratch_shapes=scratch, compiler_params=_cparams(2),
        name="lru_seg4" if seg4 else "lru",
    )(*args)


NO_RANK = 31.0


def _top_rows(vals_scr, top_scr, n_rows, rank_scr=None):
    if rank_scr is not None:
        rank_scr[...] = jnp.full(rank_scr.shape, NO_RANK, F32)

    def body(r, carry):
        v = vals_scr[0:n_rows]
        m = jnp.max(v, axis=0, keepdims=True)
        top_scr[pl.ds(r, 1), :] = m
        hit = v == m
        if rank_scr is not None:
            rank_scr[...] = jnp.where(hit, lax.convert_element_type(r, F32), rank_scr[...])
        vals_scr[0:n_rows] = jnp.where(hit, NEG_INF, v)
        return carry
    lax.fori_loop(0, TOPK, body, 0)


def _peer_retrieve_kernel(y_ref, g_ref, wq_ref, keys_ref,
                          hbt_ref, cnt_ref, rr_ref, rk2_ref, p2_ref,
                          vals_scr, ta_scr, tb_scr, cand_scr, top_scr, rank_scr):
    tm = y_ref.shape[0]
    h = _rmsnorm(y_ref[...], g_ref[...])
    hbt_ref[...] = jnp.transpose(h).astype(BF16)
    hb = h.astype(BF16)
    qb = _dot(hb, wq_ref[...]).astype(BF16)
    row8 = lax.broadcasted_iota(jnp.int32, (SUBLANES, tm), 0)
    for hh in range(PEER_HEADS):
        c0 = hh * 2 * N_KEYS
        s1 = _dot_nt(keys_ref[hh, 0], qb[:, c0:c0 + N_KEYS])
        s2 = _dot_nt(keys_ref[hh, 1], qb[:, c0 + N_KEYS:c0 + 2 * N_KEYS])
        vals_scr[...] = s1
        _top_rows(vals_scr, ta_scr, N_KEYS)
        vals_scr[...] = s2
        _top_rows(vals_scr, tb_scr, N_KEYS, rank_scr)
        a = ta_scr[0:TOPK]
        b = tb_scr[0:TOPK]
        a0, b0 = a[0:1], b[0:1]
        a_lo, b_lo = a[0:SUBLANES], b[0:SUBLANES]
        cand_scr[0:16] = a0 + b
        cand_scr[16:24] = a[1:2] + b_lo
        cand_scr[24:32] = jnp.where(row8 < 5, a[2:3] + b_lo, NEG_INF)
        cand_scr[32:40] = jnp.where(row8 < 4, a[3:4] + b_lo, NEG_INF)
        cand_scr[40:48] = jnp.where(row8 < 3, a[4:5] + b_lo, NEG_INF)
        cand_scr[48:56] = a[SUBLANES:2 * SUBLANES] + b0
        cand_scr[56:64] = jnp.where(row8 >= 5, a_lo + b0, NEG_INF)
        cand_scr[64:72] = jnp.where(row8 >= 5, a_lo + b[1:2], NEG_INF)
        cand = cand_scr[...]
        vals_scr[0:72] = cand
        _top_rows(vals_scr, top_scr, 72)
        tau = top_scr[TOPK - 1:TOPK]
        z = jnp.sum(jnp.where(cand >= tau, jnp.exp(cand - (a0 + b0)), 0.0), axis=0, keepdims=True)
        cnt = jnp.zeros_like(s1)
        for r in range(TOPK):
            cnt = cnt + jnp.where(s1 + b[r:r + 1] >= tau, 1.0, 0.0)
        rr = jnp.exp(s1 - a0 - jnp.log(z))
        for p in range(N_KEYS // SUBLANES):
            cnt_ref[hh, p] = cnt[p * SUBLANES:(p + 1) * SUBLANES]
            rr_ref[hh, p] = rr[p * SUBLANES:(p + 1) * SUBLANES]
        rk2_ref[hh] = rank_scr[...].astype(BF16)
        p2_ref[hh] = jnp.exp(s2 - b0).astype(BF16)


def _peer_retrieve(y, g, w_q, keys, *, tm):
    n = y.shape[0]
    k1_shape = jax.ShapeDtypeStruct((PEER_HEADS, N_KEYS // SUBLANES, SUBLANES, n), F32)
    k2_shape = jax.ShapeDtypeStruct((PEER_HEADS, N_KEYS, n), BF16)
    k1_spec = pl.BlockSpec((PEER_HEADS, N_KEYS // SUBLANES, SUBLANES, tm), lambda i: (0, 0, 0, i))
    fac_spec = pl.BlockSpec((PEER_HEADS, N_KEYS, tm), lambda i: (0, 0, i))
    return pl.pallas_call(
        _peer_retrieve_kernel, grid=(n // tm,),
        in_specs=[pl.BlockSpec((tm, D_MODEL), lambda i: (i, 0)),
                  pl.BlockSpec((1, D_MODEL), lambda i: (0, 0)),
                  _resident((D_MODEL, D_MODEL), lambda i: (0, 0)),
                  pl.BlockSpec((PEER_HEADS, 2, N_KEYS, N_KEYS), lambda i: (0, 0, 0, 0))],
        out_specs=(pl.BlockSpec((D_MODEL, tm), lambda i: (0, i)),
                   k1_spec, k1_spec, fac_spec, fac_spec),
        out_shape=(jax.ShapeDtypeStruct((D_MODEL, n), BF16),
                   k1_shape, k1_shape, k2_shape, k2_shape),
        scratch_shapes=[pltpu.VMEM((N_KEYS, tm), F32), pltpu.VMEM((24, tm), F32),
                        pltpu.VMEM((24, tm), F32), pltpu.VMEM((72, tm), F32),
                        pltpu.VMEM((24, tm), F32), pltpu.VMEM((N_KEYS, tm), F32)],
        compiler_params=_cparams(1), name="peer_retrieve",
    )(y, g, w_q, keys)


def _peer_dense_kernel(*refs, eb, final_norm):
    if final_norm:
        (hbt_ref, y_ref, u_ref, vlo_ref, vhi_ref, cnt_ref, rr_ref, cntn_ref, rrn_ref, rk2_ref, p2_ref,
         gf_ref, o_ref, wa_scr, wb_scr, ga_scr, gb_scr, bc_scr) = refs
    else:
        (hbt_ref, y_ref, u_ref, vlo_ref, vhi_ref, cnt_ref, rr_ref, cntn_ref, rrn_ref, rk2_ref, p2_ref,
         o_ref, wa_scr, wb_scr, ga_scr, gb_scr, bc_scr) = refs
    j = pl.program_id(1)
    last = pl.num_programs(1) - 1
    n_sub = eb // N_KEYS
    tm = o_ref.shape[0]

    def gates(cnt_blk, rr_blk, half, dst_scr):
        for hh in range(PEER_HEADS):
            for i in range(n_sub):
                r = half * n_sub + i
                k = (hh * n_sub + i) * BF16_ROWS
                bc_scr[0, k:k + BF16_ROWS, :] = jnp.broadcast_to(
                    cnt_blk[hh, 0, r:r + 1, :].astype(BF16), (BF16_ROWS, tm))
                bc_scr[1, k:k + BF16_ROWS, :] = jnp.broadcast_to(
                    rr_blk[hh, 0, r:r + 1, :].astype(BF16), (BF16_ROWS, tm))
        for i in range(n_sub):
            for c in range(N_KEYS // BF16_ROWS):
                rows = slice(c * BF16_ROWS, (c + 1) * BF16_ROWS)
                gate = jnp.zeros((BF16_ROWS, tm), BF16)
                for hh in range(PEER_HEADS):
                    k = (hh * n_sub + i) * BF16_ROWS
                    cnt = bc_scr[0, k:k + BF16_ROWS, :]
                    rr = bc_scr[1, k:k + BF16_ROWS, :]
                    gate = gate + jnp.where(rk2_ref[hh, rows, :] < cnt, p2_ref[hh, rows, :] * rr,
                                            jnp.zeros((), BF16))
                dst_scr[i * N_KEYS + c * BF16_ROWS:i * N_KEYS + (c + 1) * BF16_ROWS, :] = gate

    def up(half, g_scr, dst_scr):
        act = _gelu(_dot(u_ref[half * eb:(half + 1) * eb, :], hbt_ref[...]))
        dst_scr[...] = act.astype(BF16) * g_scr[...]

    @pl.when(j == 0)
    def _():
        o_ref[...] = y_ref[...]
        wb_scr[...] = jnp.zeros_like(wb_scr)
        gates(cnt_ref, rr_ref, 0, ga_scr)

    o_ref[...] += _dot_tn(wb_scr[...], vlo_ref[...])
    up(0, ga_scr, wa_scr)
    gates(cnt_ref, rr_ref, 1, gb_scr)

    @pl.when(j < last)
    def _():
        o_ref[...] += _dot_tn(wa_scr[...], vhi_ref[...])
        up(1, gb_scr, wb_scr)
        gates(cntn_ref, rrn_ref, 0, ga_scr)

    if final_norm:
        @pl.when(j == last)
        def _():
            o_ref[...] = _rmsnorm(o_ref[...], gf_ref[...])


def _peer_dense(hbt, y, u, v, cnt, rr, rk2, p2, *, layer, tm, eb, g_final=None):
    n = y.shape[0]
    n_blk = N_EXPERTS // eb
    n_pair = n_blk // 2
    n_sub = eb // N_KEYS
    assert cnt.shape == (PEER_HEADS, n_pair, 2 * n_sub, n), (cnt.shape, eb)
    tile = pl.BlockSpec((tm, D_MODEL), lambda i, j: (i, 0))
    fac_blk = pl.BlockSpec((PEER_HEADS, 1, 2 * n_sub, tm),
                           lambda i, j: (0, jnp.minimum(j, n_pair - 1), 0, i))
    fac_next = pl.BlockSpec((PEER_HEADS, 1, 2 * n_sub, tm),
                            lambda i, j: (0, jnp.minimum(j + 1, n_pair - 1), 0, i))
    in_specs = [
        _resident((D_MODEL, tm), lambda i, j: (0, i)),
        _resident((tm, D_MODEL), lambda i, j: (i, 0)),
        pl.BlockSpec((None, 2 * eb, D_MODEL), lambda i, j: (layer, jnp.minimum(j, n_pair - 1), 0)),
        pl.BlockSpec((None, eb, D_MODEL), lambda i, j: (layer, jnp.maximum(2 * j - 1, 0), 0)),
        pl.BlockSpec((None, eb, D_MODEL), lambda i, j: (layer, jnp.minimum(2 * j, n_blk - 1), 0)),
        fac_blk, fac_blk, fac_next, fac_next,
        _resident((PEER_HEADS, N_KEYS, tm), lambda i, j: (0, 0, i)),
        _resident((PEER_HEADS, N_KEYS, tm), lambda i, j: (0, 0, i)),
    ]
    args = [hbt, y, u, v, v, cnt, rr, cnt, rr, rk2, p2]
    if g_final is not None:
        in_specs.append(pl.BlockSpec((1, D_MODEL), lambda i, j: (0, 0)))
        args.append(g_final)
    return pl.pallas_call(
        functools.partial(_peer_dense_kernel, eb=eb, final_norm=g_final is not None),
        grid=(n // tm, n_pair + 1), in_specs=in_specs, out_specs=tile,
        out_shape=jax.ShapeDtypeStruct((n, D_MODEL), F32),
        scratch_shapes=[pltpu.VMEM((eb, tm), BF16)] * 4
        + [pltpu.VMEM((2, PEER_HEADS * n_sub * BF16_ROWS, tm), BF16)],
        compiler_params=_cparams(2), name="peer_dense",
    )(*args)


def _peer(y, g, w_q, keys, u, v, *, layer, tm_retrieve, tm, eb, g_final=None):
    hbt, cnt, rr, rk2, p2 = _peer_retrieve(y, g, w_q, keys, tm=tm_retrieve)
    return _peer_dense(hbt, y, u, v, cnt, rr, rk2, p2, layer=layer, tm=tm, eb=eb, g_final=g_final)


PEER_TM = 512
PEER_DENSE_TM = 512
PEER_EB = 512
SEQ_ROWS = 256
S5_ROWS = 512
S5_LONG_ROWS = 128
CAST_ROWS = 1024


def _row(v):
    return v.reshape(1, -1)


@jax.jit
def _step(x_prompt, x_sample, state_s5_re, state_s5_im, state_lru_h, state_lru_conv,
          norm_mix, norm_ffn, norm_final,
          s5_lam_re, s5_lam_im, s5_log_dt, s5_b_re, s5_b_im, s5_c_re, s5_c_im, s5_d, s5_w_glu, s5_b_glu,
          lru_w_in, lru_conv_w, lru_conv_b, lru_w_a, lru_b_a, lru_w_i, lru_b_i, lru_lam, lru_w_out,
          peer_w_q, peer_sub_keys, peer_u, peer_v):
    bsz, seq, d = x_prompt.shape
    dec_b, dec_t, _ = x_sample.shape
    xp = x_prompt.reshape(bsz * seq, d)
    xs = x_sample.reshape(dec_b * dec_t, d)
    n_s = dec_b * dec_t

    pow_re, pow_im, bb_re, bb_im = _s5_discretize(
        s5_lam_re[0], s5_lam_im[0], s5_log_dt[0], s5_b_re[0], s5_b_im[0])
    g_mix0 = _row(norm_mix[0])
    w_glu = s5_w_glu[0].astype(BF16)
    wb, wc = _s5_proj_weights(bb_re, bb_im, s5_c_re[0], s5_c_im[0])

    def glu(x, ymix):
        return _s5_glu(x, g_mix0, ymix, _row(s5_d[0]), w_glu, _row(s5_b_glu[0]), tm=PEER_TM)

    ymix, sre, sim = _s5_scan_long(
        x_prompt, g_mix0, wb, wc,
        pow_re[0].reshape(S5_KB, SUBLANES, LANES), pow_im[0].reshape(S5_KB, SUBLANES, LANES),
        rows=S5_LONG_ROWS)
    yp = glu(xp, ymix.reshape(bsz * seq, d))

    def final(s):
        return jnp.transpose(s, (1, 0, 2, 3)).reshape(1, bsz, S5_GROUPS, S5_STATE)
    s5_p = (final(sre), final(sim))

    assert dec_t in (1, 2, 4) and n_s % S5_ROWS == 0

    def rep(s):
        s = jnp.transpose(s.reshape(dec_b, S5_KB, S5_SW), (1, 0, 2))
        return jnp.repeat(s, dec_t, axis=1)
    ymix, sre, sim = _s5_scan_short(
        xs, g_mix0, wb, wc, _s5_short_consts(pow_re, dec_t), _s5_short_consts(pow_im, dec_t),
        (rep(state_s5_re[0]), rep(state_s5_im[0])), rows=S5_ROWS)
    ys = glu(xs, ymix)

    def last(s):
        s = s.reshape(S5_KB, dec_b, dec_t, S5_SW)[:, :, dec_t - 1, :]
        return jnp.transpose(s, (1, 0, 2)).reshape(1, dec_b, S5_GROUPS, S5_STATE)
    s5_s = (last(sre), last(sim))

    u_bf = _cast_bf16(peer_u, rows=CAST_ROWS)
    v_bf = _cast_bf16(peer_v, rows=CAST_ROWS)

    def peer_layer(i, y, g_final=None):
        return _peer(y, _row(norm_ffn[i]), peer_w_q[i].astype(BF16), peer_sub_keys[i].astype(BF16),
                     u_bf, v_bf, layer=i,
                     tm_retrieve=PEER_TM, tm=min(PEER_DENSE_TM, y.shape[0]), eb=PEER_EB, g_final=g_final)
    yp = peer_layer(0, yp)
    ys = peer_layer(0, ys)

    g_mix1 = _row(norm_mix[1])
    w_in = lru_w_in[0].astype(BF16)
    w_out = lru_w_out[0].astype(BF16)
    gate_args = (lru_conv_w[0], _row(lru_conv_b[0]), lru_w_a[0].astype(BF16), _row(lru_b_a[0]),
                 lru_w_i[0].astype(BF16), _row(lru_b_i[0]), _row(lru_lam[0]))

    zp = _matmul(yp, w_in, tm=PEER_TM, tn=D_MODEL, g=g_mix1, name="lru_in_proj")
    gated_p, hst_p, xbst_p = _lru(zp, *gate_args, n_seq=bsz, rows=SEQ_ROWS)
    yp = _matmul(gated_p, w_out, tm=PEER_TM, tn=D_MODEL, res=yp, name="lru_out_proj")
    lru_h_p = hst_p[:, 0, :].reshape(1, bsz, d)
    lru_c_p = xbst_p[:, SUBLANES - (CONV_WIDTH - 1):, :].reshape(1, bsz, CONV_WIDTH - 1, d)

    buf = state_lru_conv[0]
    zero = jnp.zeros((dec_b, 1, d), F32)
    pcv = jnp.stack([
        jnp.concatenate([buf[:, 2:3], zero, zero, zero], axis=1),
        jnp.concatenate([buf[:, 1:3], zero, zero], axis=1),
        jnp.concatenate([buf[:, 0:3], zero], axis=1),
    ]).reshape(CONV_WIDTH - 1, n_s, d)
    h0 = jnp.repeat(state_lru_h[0], dec_t, axis=0)
    zs = _matmul(ys, w_in, tm=PEER_TM, tn=D_MODEL, g=g_mix1, name="lru_in_proj")
    gated_s, hst_s, xbst_s = _lru(zs, *gate_args, n_seq=n_s // SEQ_ROWS, rows=SEQ_ROWS, pcv=pcv, h0=h0)
    ys = _matmul(gated_s, w_out, tm=PEER_TM, tn=D_MODEL, res=ys, name="lru_out_proj")
    lru_h_s = hst_s.reshape(dec_b, dec_t, d)[:, dec_t - 1].reshape(1, dec_b, d)
    lru_c_s = xbst_s.reshape(dec_b, dec_t, d)[:, 1:].reshape(1, dec_b, CONV_WIDTH - 1, d)

    g_fin = _row(norm_final)
    yp = peer_layer(1, yp, g_final=g_fin)
    ys = peer_layer(1, ys, g_final=g_fin)

    return (yp.reshape(bsz, seq, d), ys.reshape(dec_b, dec_t, d),
            s5_p[0], s5_p[1], s5_s[0], s5_s[1],
            lru_h_p, lru_c_p, lru_h_s, lru_c_s)


def kernel(x_prompt, x_sample, state_s5_re, state_s5_im, state_lru_h, state_lru_conv, norm_mix, norm_ffn, norm_final, s5_lam_re, s5_lam_im, s5_log_dt, s5_b_re, s5_b_im, s5_c_re, s5_c_im, s5_d, s5_w_glu, s5_b_glu, lru_w_in, lru_conv_w, lru_conv_b, lru_w_a, lru_b_a, lru_w_i, lru_b_i, lru_lam, lru_w_out, peer_w_q, peer_sub_keys, peer_u, peer_v):
    return _step(x_prompt, x_sample, state_s5_re, state_s5_im, state_lru_h, state_lru_conv,
                 norm_mix, norm_ffn, norm_final,
                 s5_lam_re, s5_lam_im, s5_log_dt, s5_b_re, s5_b_im, s5_c_re, s5_c_im, s5_d, s5_w_glu, s5_b_glu,
                 lru_w_in, lru_conv_w, lru_conv_b, lru_w_a, lru_b_a, lru_w_i, lru_b_i, lru_lam, lru_w_out,
                 peer_w_q, peer_sub_keys, peer_u, peer_v)
```

```python
import functools
import math

import jax
import jax.numpy as jnp
from jax import lax
from jax.experimental import pallas as pl
from jax.experimental.pallas import tpu as pltpu

F32 = jnp.float32
BF16 = jnp.bfloat16

D_MODEL = 2048
RMS_EPS = 1e-6
S5_GROUP = 16
S5_GROUPS = 128
S5_STATE = 64
S5_KB = 8
S5_KBW = 256
S5_SW = 1024
LRU_HEADS = 8
LRU_BLOCK = 256
CONV_WIDTH = 4
LRU_C = 8.0
PEER_HEADS = 8
N_KEYS = 128
N_EXPERTS = N_KEYS * N_KEYS
TOPK = 16
SUBLANES = 8
LANES = 128
BF16_ROWS = 2 * SUBLANES
VMEM_LIMIT_BYTES = 56 * 1024 * 1024

NEG_INF = float("-inf")


def _cparams(n_axes, flags=None):
    return pltpu.CompilerParams(
        dimension_semantics=("arbitrary",) * n_axes,
        vmem_limit_bytes=VMEM_LIMIT_BYTES,
        flags=flags,
    )


def _resident(block_shape, index_map):
    return pl.BlockSpec(block_shape, index_map, pipeline_mode=pl.Buffered(1))


def _rmsnorm(x, g):
    ms = jnp.mean(x * x, axis=-1, keepdims=True)
    return x * lax.rsqrt(ms + RMS_EPS) * g


def _gelu(x):
    return 0.5 * x * (1.0 + lax.erf(x * (1.0 / math.sqrt(2.0))))


def _dot(a, b):
    return jnp.dot(a, b, preferred_element_type=F32)


def _dot_nt(a, b):
    return lax.dot_general(a, b, (((1,), (1,)), ((), ())), preferred_element_type=F32)


def _dot_tn(a, b):
    return lax.dot_general(a, b, (((0,), (0,)), ((), ())), preferred_element_type=F32)


def _cast_kernel(x_ref, o_ref):
    o_ref[...] = x_ref[...].astype(BF16)


def _cast_bf16(x, *, rows):
    layers, n, d = x.shape
    spec = pl.BlockSpec((None, rows, d), lambda l, i: (l, i, 0))
    return pl.pallas_call(
        _cast_kernel, grid=(layers, n // rows), in_specs=[spec], out_specs=spec,
        out_shape=jax.ShapeDtypeStruct((layers, n, d), BF16),
        compiler_params=_cparams(2), name="cast_bf16",
    )(x)


def _s5_discretize_kernel(lre_ref, lim_ref, ldt_ref, bre_ref, bim_ref,
                          pre_ref, pim_ref, bbre_ref, bbim_ref):
    lr = lre_ref[...]
    li = lim_ref[...]
    dt = jnp.exp(ldt_ref[...])
    mag = jnp.exp(lr * dt)
    ab_re = mag * jnp.cos(li * dt)
    ab_im = mag * jnp.sin(li * dt)
    nr, ni = ab_re - 1.0, ab_im
    den = lr * lr + li * li
    f_re = (nr * lr + ni * li) / den
    f_im = (ni * lr - nr * li) / den
    for c in range(S5_GROUP):
        br = bre_ref[c]
        bi = bim_ref[c]
        bbre_ref[c] = f_re * br - f_im * bi
        bbim_ref[c] = f_re * bi + f_im * br
    p_re, p_im = ab_re, ab_im
    for k in range(SUBLANES):
        pre_ref[k] = p_re
        pim_ref[k] = p_im
        p_re, p_im = p_re * ab_re - p_im * ab_im, p_re * ab_im + p_im * ab_re


def _s5_discretize(lam_re, lam_im, log_dt, b_re, b_im):
    g, p = S5_GROUPS, S5_STATE
    b_re_t = jnp.transpose(b_re, (2, 0, 1))
    b_im_t = jnp.transpose(b_im, (2, 0, 1))
    out_shape = (
        jax.ShapeDtypeStruct((SUBLANES, g, p), F32),
        jax.ShapeDtypeStruct((SUBLANES, g, p), F32),
        jax.ShapeDtypeStruct((S5_GROUP, g, p), F32),
        jax.ShapeDtypeStruct((S5_GROUP, g, p), F32),
    )
    return pl.pallas_call(_s5_discretize_kernel, out_shape=out_shape, name="s5_discretize")(
        lam_re, lam_im, log_dt.reshape(g, 1), b_re_t, b_im_t)


def _s5_proj_weights(bb_re, bb_im, c_re, c_im):
    eye = jnp.eye(S5_GROUP, dtype=F32)

    def in_proj(bb):
        bb = bb.reshape(S5_GROUP, S5_KB, S5_GROUP, S5_STATE)
        w = jnp.einsum("ckgp,gh->kgchp", bb, eye)
        return w.reshape(S5_KB, S5_KBW, S5_SW)

    def out_proj(c):
        c = c.reshape(S5_KB, S5_GROUP, S5_GROUP, S5_STATE)
        w = jnp.einsum("kgcp,gh->kgphc", c, eye)
        return w.reshape(S5_KB, S5_SW, S5_KBW)

    wb = jnp.concatenate([in_proj(bb_re), in_proj(bb_im)], axis=2).astype(BF16)
    wc = jnp.concatenate([out_proj(c_re), out_proj(-c_im)], axis=1).astype(BF16)
    return wb, wc


def _s5_short_consts(pw, seg):
    pos = jnp.arange(SUBLANES) % seg
    pw = pw.reshape(SUBLANES, S5_KB, S5_SW)
    steps = [pw[d - 1][:, None, :] * (pos >= d).astype(F32)[None, :, None] for d in (1, 2)]
    init = jnp.transpose(pw[pos], (1, 0, 2))
    return jnp.stack(steps + [init], axis=1)


def _s5_scan_short_kernel(x_ref, g_ref, wb_ref, wc_ref, cre_ref, cim_ref, h0re_ref, h0im_ref,
                          ymix_ref, sre_ref, sim_ref, hb_scr, bu_scr, *, seg):
    kb = pl.program_id(1)
    rows = x_ref.shape[0]
    n_sq = SUBLANES // seg
    pos8 = lax.broadcasted_iota(jnp.int32, (SUBLANES, S5_SW), 0)

    def per_row(h_ref, sq):
        def seq_row(q):
            return jnp.broadcast_to(h_ref[0, pl.ds(sq + q, 1), :], (SUBLANES, S5_SW))
        out = seq_row(n_sq - 1)
        for q in range(n_sq - 2, -1, -1):
            out = jnp.where(pos8 < (q + 1) * seg, seq_row(q), out)
        return out

    @pl.when(kb == 0)
    def _():
        hb = _rmsnorm(x_ref[...], g_ref[...]).astype(BF16)
        for j in range(S5_KB):
            hb_scr[j] = hb[:, j * S5_KBW:(j + 1) * S5_KBW]

    bu_scr[...] = _dot(hb_scr[kb], wb_ref[0])

    def body(r, carry):
        row = pl.multiple_of(r * SUBLANES, SUBLANES)
        re = bu_scr[pl.ds(row, SUBLANES), 0:S5_SW]
        im = bu_scr[pl.ds(row, SUBLANES), S5_SW:2 * S5_SW]
        for idx, d in enumerate((1, 2)):
            ar = cre_ref[0, idx]
            ai = cim_ref[0, idx]
            sr = pltpu.roll(re, d, 0)
            si = pltpu.roll(im, d, 0)
            re, im = re + ar * sr - ai * si, im + ar * si + ai * sr
        pr = cre_ref[0, 2]
        pi = cim_ref[0, 2]
        sq = r * n_sq
        cr = per_row(h0re_ref, sq)
        ci = per_row(h0im_ref, sq)
        re, im = re + pr * cr - pi * ci, im + pr * ci + pi * cr
        bu_scr[pl.ds(row, SUBLANES), 0:S5_SW] = re
        bu_scr[pl.ds(row, SUBLANES), S5_SW:2 * S5_SW] = im
        for q in range(n_sq):
            last = (q + 1) * seg - 1
            sre_ref[0, pl.ds(sq + q, 1), :] = re[last:last + 1]
            sim_ref[0, pl.ds(sq + q, 1), :] = im[last:last + 1]
        return carry

    lax.fori_loop(0, rows // SUBLANES, body, 0)
    ymix_ref[...] = _dot(bu_scr[...].astype(BF16), wc_ref[0])


def _s5_scan_short(x, g, wb, wc, cre, cim, h0, *, rows, seg):
    n = x.shape[0]
    c_spec = pl.BlockSpec((1, 3, SUBLANES, S5_SW), lambda i, k: (k, 0, 0, 0))
    st_spec = pl.BlockSpec((1, rows // seg, S5_SW), lambda i, k: (k, i, 0))
    st_shape = jax.ShapeDtypeStruct((S5_KB, n // seg, S5_SW), F32)
    return pl.pallas_call(
        functools.partial(_s5_scan_short_kernel, seg=seg), grid=(n // rows, S5_KB),
        in_specs=[pl.BlockSpec((rows, D_MODEL), lambda i, k: (i, 0)),
                  pl.BlockSpec((1, D_MODEL), lambda i, k: (0, 0)),
                  pl.BlockSpec((1, S5_KBW, 2 * S5_SW), lambda i, k: (k, 0, 0)),
                  pl.BlockSpec((1, 2 * S5_SW, S5_KBW), lambda i, k: (k, 0, 0)),
                  c_spec, c_spec, st_spec, st_spec],
        out_specs=(pl.BlockSpec((rows, S5_KBW), lambda i, k: (i, k)), st_spec, st_spec),
        out_shape=(jax.ShapeDtypeStruct((n, D_MODEL), F32), st_shape, st_shape),
        scratch_shapes=[pltpu.VMEM((S5_KB, rows, S5_KBW), BF16), pltpu.VMEM((rows, 2 * S5_SW), F32)],
        compiler_params=_cparams(2), name="s5_scan_short",
    )(x, g, wb, wc, cre, cim, h0[0], h0[1])


S5_CT = 2 * S5_SW // LANES


def _s5_scan_long_kernel(x_ref, g_ref, wb_ref, wc_ref, are_ref, aim_ref,
                         ymix_ref, sre_ref, sim_ref, hb_scr, t_scr, carry_scr):
    tc = pl.program_id(0)
    kb = pl.program_id(1)
    n_seq, rows = x_ref.shape[0], x_ref.shape[1]
    n_grp = rows // SUBLANES
    half = S5_CT // 2

    @pl.when(kb == 0)
    def _():
        for b in range(n_seq):
            hb = _rmsnorm(x_ref[b], g_ref[...]).astype(BF16)
            for j in range(S5_KB):
                hb_scr[j, b * rows:(b + 1) * rows, :] = hb[:, j * S5_KBW:(j + 1) * S5_KBW]

    @pl.when(jnp.logical_and(tc == 0, kb == 0))
    def _():
        carry_scr[...] = jnp.zeros_like(carry_scr)

    bu = _dot(hb_scr[kb], wb_ref[0])
    for b in range(n_seq):
        for g in range(n_grp):
            r0 = b * rows + g * SUBLANES
            for ct in range(S5_CT):
                t0 = (g * S5_CT + ct) * SUBLANES
                t_scr[b, t0:t0 + SUBLANES, :] = bu[r0:r0 + SUBLANES, ct * LANES:(ct + 1) * LANES]

    ar = are_ref[0]
    ai = aim_ref[0]

    def body(g, carry):
        carry = list(carry)
        for s in range(SUBLANES):
            for b in range(n_seq):
                i_re = pl.ds(g * (S5_CT * SUBLANES) + s, SUBLANES, stride=SUBLANES)
                i_im = pl.ds(g * (S5_CT * SUBLANES) + half * SUBLANES + s, SUBLANES, stride=SUBLANES)
                sr, si = carry[2 * b], carry[2 * b + 1]
                sr, si = (ar * sr - ai * si + t_scr[b, i_re, :],
                          ar * si + ai * sr + t_scr[b, i_im, :])
                t_scr[b, i_re, :] = sr
                t_scr[b, i_im, :] = si
                carry[2 * b], carry[2 * b + 1] = sr, si
        return tuple(carry)

    c0 = tuple(carry_scr[kb, i] for i in range(2 * n_seq))
    c = lax.fori_loop(0, n_grp, body, c0)
    for b in range(n_seq):
        carry_scr[kb, 2 * b] = c[2 * b]
        carry_scr[kb, 2 * b + 1] = c[2 * b + 1]
        sre_ref[kb, b] = c[2 * b]
        sim_ref[kb, b] = c[2 * b + 1]

    row_blocks = []
    for b in range(n_seq):
        for g in range(n_grp):
            tiles = [t_scr[b, (g * S5_CT + ct) * SUBLANES:(g * S5_CT + ct + 1) * SUBLANES, :]
                     for ct in range(S5_CT)]
            row_blocks.append(jnp.concatenate(tiles, axis=1))
    states = jnp.concatenate(row_blocks, axis=0).astype(BF16)
    ymix = _dot(states, wc_ref[0])
    for b in range(n_seq):
        ymix_ref[b] = ymix[b * rows:(b + 1) * rows]


def _s5_scan_long(x, g, wb, wc, a_re, a_im, *, rows):
    n_seq, t_len, _ = x.shape
    grid = (t_len // rows, S5_KB)
    st_shape = jax.ShapeDtypeStruct((S5_KB, n_seq, SUBLANES, LANES), F32)
    st_spec = pl.BlockSpec((S5_KB, n_seq, SUBLANES, LANES), lambda t, k: (0, 0, 0, 0))
    a_spec = pl.BlockSpec((1, SUBLANES, LANES), lambda t, k: (k, 0, 0))
    return pl.pallas_call(
        _s5_scan_long_kernel, grid=grid,
        in_specs=[pl.BlockSpec((n_seq, rows, D_MODEL), lambda t, k: (0, t, 0)),
                  pl.BlockSpec((1, D_MODEL), lambda t, k: (0, 0)),
                  pl.BlockSpec((1, S5_KBW, 2 * S5_SW), lambda t, k: (k, 0, 0)),
                  pl.BlockSpec((1, 2 * S5_SW, S5_KBW), lambda t, k: (k, 0, 0)),
                  a_spec, a_spec],
        out_specs=(pl.BlockSpec((n_seq, rows, S5_KBW), lambda t, k: (0, t, k)), st_spec, st_spec),
        out_shape=(jax.ShapeDtypeStruct((n_seq, t_len, D_MODEL), F32), st_shape, st_shape),
        scratch_shapes=[pltpu.VMEM((S5_KB, n_seq * rows, S5_KBW), BF16),
                        pltpu.VMEM((n_seq, rows * S5_CT, LANES), F32),
                        pltpu.VMEM((S5_KB, 2 * n_seq, SUBLANES, LANES), F32)],
        compiler_params=_cparams(2), name="s5_scan_long",
    )(x, g, wb, wc, a_re, a_im)


def _s5_glu_kernel(x_ref, g_ref, ymix_ref, d_ref, w_ref, b_ref, o_ref):
    x = x_ref[...]
    h = _rmsnorm(x, g_ref[...])
    y = _gelu(ymix_ref[...] + d_ref[...] * h)
    z = _dot(y.astype(BF16), w_ref[...]) + b_ref[...]
    o_ref[...] = x + y * jax.nn.sigmoid(z)


def _s5_glu(x, g, ymix, d_skip, w_glu, b_glu, *, tm):
    n = x.shape[0]
    tile = pl.BlockSpec((tm, D_MODEL), lambda i: (i, 0))
    vec = pl.BlockSpec((1, D_MODEL), lambda i: (0, 0))
    return pl.pallas_call(
        _s5_glu_kernel, grid=(n // tm,),
        in_specs=[tile, vec, tile, vec, _resident((D_MODEL, D_MODEL), lambda i: (0, 0)), vec],
        out_specs=tile, out_shape=jax.ShapeDtypeStruct((n, D_MODEL), F32),
        compiler_params=_cparams(1), name="s5_glu",
    )(x, g, ymix, d_skip, w_glu, b_glu)


def _matmul_kernel(*refs, norm, residual):
    refs = list(refs)
    x_ref = refs.pop(0)
    g_ref = refs.pop(0) if norm else None
    w_ref = refs.pop(0)
    r_ref = refs.pop(0) if residual else None
    o_ref = refs.pop(0)
    x = x_ref[...]
    if norm:
        x = _rmsnorm(x, g_ref[...]).astype(BF16)
    acc = _dot(x, w_ref[...])
    if residual:
        acc = acc + r_ref[...]
    o_ref[...] = acc


def _matmul(x, w, *, tm, tn, g=None, res=None, name):
    n, k = x.shape
    n_out = w.shape[1]
    grid = (n_out // tn, n // tm)
    in_specs = [pl.BlockSpec((tm, k), lambda j, i: (i, 0))]
    args = [x]
    if g is not None:
        in_specs.append(pl.BlockSpec((1, k), lambda j, i: (0, 0)))
        args.append(g)
    in_specs.append(pl.BlockSpec((k, tn), lambda j, i: (0, j)))
    args.append(w)
    if res is not None:
        in_specs.append(pl.BlockSpec((tm, tn), lambda j, i: (i, j)))
        args.append(res)
    return pl.pallas_call(
        functools.partial(_matmul_kernel, norm=g is not None, residual=res is not None),
        grid=grid, in_specs=in_specs,
        out_specs=pl.BlockSpec((tm, tn), lambda j, i: (i, j)),
        out_shape=jax.ShapeDtypeStruct((n, n_out), F32),
        compiler_params=_cparams(2), name=name,
    )(*args)


def _lru_kernel(*refs, seg4):
    if seg4:
        (z_ref, cw_ref, cb_ref, wa_ref, ba_ref, wi_ref, bi_ref, lam_ref, pcv_ref, h0_ref,
         gated_ref, hst_ref, xbst_ref, a_scr, b_scr) = refs
    else:
        (z_ref, cw_ref, cb_ref, wa_ref, ba_ref, wi_ref, bi_ref, lam_ref,
         gated_ref, hst_ref, xbst_ref, a_scr, b_scr, prev_scr, carry_scr) = refs
    tc = pl.program_id(1)
    rows = z_ref.shape[0]
    d = D_MODEL
    gate = z_ref[:, 0:d]
    xb = z_ref[:, d:2 * d]

    row8 = lax.broadcasted_iota(jnp.int32, (SUBLANES, d), 0)
    if seg4:
        t_full = lax.broadcasted_iota(jnp.int32, (rows, d), 0) % CONV_WIDTH
        xc = cb_ref[...] + cw_ref[3:4, :] * xb
        for s in range(1, CONV_WIDTH):
            shifted = jnp.where(t_full >= s, pltpu.roll(xb, s, 0), pcv_ref[s - 1])
            xc = xc + cw_ref[3 - s:4 - s, :] * shifted
        xbst_ref[...] = xb
        rseg = row8 % CONV_WIDTH
    else:
        @pl.when(tc == 0)
        def _():
            prev_scr[...] = jnp.zeros_like(prev_scr)
            carry_scr[...] = jnp.zeros_like(carry_scr)
        xcat = jnp.concatenate([prev_scr[...], xb], axis=0)
        xc = cb_ref[...] + cw_ref[3:4, :] * xb
        for s in range(1, CONV_WIDTH):
            xc = xc + cw_ref[3 - s:4 - s, :] * pltpu.roll(xcat, s, 0)[SUBLANES:]
        tail = xb[rows - SUBLANES:]
        prev_scr[...] = tail
        xbst_ref[0] = tail
        rseg = row8

    r_parts, i_parts = [], []
    for hh in range(LRU_HEADS):
        xh = xc[:, hh * LRU_BLOCK:(hh + 1) * LRU_BLOCK].astype(BF16)
        r_parts.append(_dot(xh, wa_ref[hh]))
        i_parts.append(_dot(xh, wi_ref[hh]))
    r = jax.nn.sigmoid(jnp.concatenate(r_parts, axis=1) + ba_ref[...])
    ig = jax.nn.sigmoid(jnp.concatenate(i_parts, axis=1) + bi_ref[...])
    log_a = -LRU_C * r * jax.nn.softplus(-lam_ref[...])
    a = jnp.exp(log_a)
    a_scr[...] = a
    b_scr[...] = jnp.sqrt(-jnp.tanh(log_a) * (a * a + 1.0)) * (ig * xc)

    masks = [rseg >= s for s in (1, 2, 4)]

    def body(g, carry):
        row = pl.multiple_of(g * SUBLANES, SUBLANES)
        av = a_scr[pl.ds(row, SUBLANES), :]
        bv = b_scr[pl.ds(row, SUBLANES), :]
        for m, s in zip(masks, (1, 2, 4)):
            if seg4 and s == 4:
                continue
            a_sh = jnp.where(m, pltpu.roll(av, s, 0), 1.0)
            b_sh = jnp.where(m, pltpu.roll(bv, s, 0), 0.0)
            bv = bv + av * b_sh
            av = av * a_sh
        c = h0_ref[pl.ds(row, SUBLANES), :] if seg4 else carry
        hv = bv + av * c
        b_scr[pl.ds(row, SUBLANES), :] = hv
        if seg4:
            return carry
        return jnp.broadcast_to(hv[SUBLANES - 1:SUBLANES], (SUBLANES, d))

    if seg4:
        lax.fori_loop(0, rows // SUBLANES, body, 0)
        hst_ref[...] = b_scr[...]
    else:
        c = lax.fori_loop(0, rows // SUBLANES, body, carry_scr[...])
        carry_scr[...] = c
        hst_ref[0] = c
    gated_ref[...] = (b_scr[...] * _gelu(gate)).astype(BF16)


def _lru(z, conv_w, conv_b, w_a, b_a, w_i, b_i, lam, *, n_seq, rows, pcv=None, h0=None):
    n = z.shape[0]
    seg4 = h0 is not None
    d = D_MODEL
    n_chunk = n // (n_seq * rows)

    def row_map(s, t):
        return (s * n_chunk + t, 0)

    vec = pl.BlockSpec((1, d), lambda s, t: (0, 0))
    gate_w = pl.BlockSpec((LRU_HEADS, LRU_BLOCK, LRU_BLOCK), lambda s, t: (0, 0, 0))
    in_specs = [pl.BlockSpec((rows, 2 * d), row_map),
                pl.BlockSpec((CONV_WIDTH, d), lambda s, t: (0, 0)), vec,
                gate_w, vec, gate_w, vec, vec]
    args = [z, conv_w, conv_b, w_a, b_a, w_i, b_i, lam]
    scratch = [pltpu.VMEM((rows, d), F32), pltpu.VMEM((rows, d), F32)]
    tile = pl.BlockSpec((rows, d), row_map)
    if seg4:
        in_specs += [pl.BlockSpec((CONV_WIDTH - 1, rows, d), lambda s, t: (0, s * n_chunk + t, 0)), tile]
        args += [pcv, h0]
        st_shape = jax.ShapeDtypeStruct((n, d), F32)
        st_spec = tile
    else:
        scratch += [pltpu.VMEM((SUBLANES, d), F32), pltpu.VMEM((SUBLANES, d), F32)]
        st_shape = jax.ShapeDtypeStruct((n_seq, SUBLANES, d), F32)
        st_spec = pl.BlockSpec((1, SUBLANES, d), lambda s, t: (s, 0, 0))
    return pl.pallas_call(
        functools.partial(_lru_kernel, seg4=seg4),
        grid=(n_seq, n_chunk), in_specs=in_specs,
        out_specs=(tile, st_spec, st_spec),
        out_shape=(jax.ShapeDtypeStruct((n, d), BF16), st_shape, st_shape),
        scratch_shapes=scratch, compiler_params=_cparams(2),
        name="lru_seg4" if seg4 else "lru",
    )(*args)


NO_RANK = 31.0


def _top_rows(vals_scr, top_scr, n_rows):
    def body(r, carry):
        v = vals_scr[0:n_rows]
        m = jnp.max(v, axis=0, keepdims=True)
        top_scr[pl.ds(r, 1), :] = m
        vals_scr[0:n_rows] = jnp.where(v == m, NEG_INF, v)
        return carry
    lax.fori_loop(0, TOPK, body, 0)


def _top_rows_pair(va_scr, ta_scr, vb_scr, tb_scr, rank_scr):
    rank_scr[...] = jnp.full(rank_scr.shape, NO_RANK, F32)

    def body(r, carry):
        va = va_scr[...]
        vb = vb_scr[...]
        ma = jnp.max(va, axis=0, keepdims=True)
        mb = jnp.max(vb, axis=0, keepdims=True)
        ta_scr[pl.ds(r, 1), :] = ma
        tb_scr[pl.ds(r, 1), :] = mb
        hit_b = vb == mb
        rank_scr[...] = jnp.where(hit_b, lax.convert_element_type(r, F32), rank_scr[...])
        va_scr[...] = jnp.where(va == ma, NEG_INF, va)
        vb_scr[...] = jnp.where(hit_b, NEG_INF, vb)
        return carry
    lax.fori_loop(0, TOPK, body, 0)


def _peer_retrieve_kernel(y_ref, g_ref, wq_ref, keys_ref,
                          hbt_ref, cnt_ref, rr_ref, rk2_ref, p2_ref,
                          vals_scr, vals2_scr, ta_scr, tb_scr, cand_scr, top_scr, rank_scr):
    tm = y_ref.shape[0]
    h = _rmsnorm(y_ref[...], g_ref[...])
    hb = h.astype(BF16)
    hbt_ref[...] = jnp.transpose(hb)
    qb = _dot(hb, wq_ref[...]).astype(BF16)
    row8 = lax.broadcasted_iota(jnp.int32, (SUBLANES, tm), 0)
    for hh in range(PEER_HEADS):
        c0 = hh * 2 * N_KEYS
        s1 = _dot_nt(keys_ref[hh, 0], qb[:, c0:c0 + N_KEYS])
        s2 = _dot_nt(keys_ref[hh, 1], qb[:, c0 + N_KEYS:c0 + 2 * N_KEYS])
        vals_scr[...] = s1
        vals2_scr[...] = s2
        _top_rows_pair(vals_scr, ta_scr, vals2_scr, tb_scr, rank_scr)
        a = ta_scr[0:TOPK]
        b = tb_scr[0:TOPK]
        a0, b0 = a[0:1], b[0:1]
        a_lo, b_lo = a[0:SUBLANES], b[0:SUBLANES]
        cand_scr[0:16] = a0 + b
        cand_scr[16:24] = a[1:2] + b_lo
        cand_scr[24:32] = jnp.where(row8 < 5, a[2:3] + b_lo, NEG_INF)
        cand_scr[32:40] = jnp.where(row8 < 4, a[3:4] + b_lo, NEG_INF)
        cand_scr[40:48] = jnp.where(row8 < 3, a[4:5] + b_lo, NEG_INF)
        cand_scr[48:56] = a[SUBLANES:2 * SUBLANES] + b0
        cand_scr[56:64] = jnp.where(row8 >= 5, a_lo + b0, NEG_INF)
        cand_scr[64:72] = jnp.where(row8 >= 5, a_lo + b[1:2], NEG_INF)
        cand = cand_scr[...]
        vals_scr[0:72] = cand
        _top_rows(vals_scr, top_scr, 72)
        tau = top_scr[TOPK - 1:TOPK]
        z = jnp.sum(jnp.where(cand >= tau, jnp.exp(cand - (a0 + b0)), 0.0), axis=0, keepdims=True)
        cnt = jnp.zeros_like(s1)
        for r in range(TOPK):
            cnt = cnt + jnp.where(s1 + b[r:r + 1] >= tau, 1.0, 0.0)
        rr = 0.5 * jnp.exp(s1 - a0 - jnp.log(z))
        for p in range(N_KEYS // SUBLANES):
            cnt_ref[hh, p] = cnt[p * SUBLANES:(p + 1) * SUBLANES]
            rr_ref[hh, p] = rr[p * SUBLANES:(p + 1) * SUBLANES]
        rk2_ref[hh] = rank_scr[...].astype(BF16)
        p2_ref[hh] = jnp.exp(s2 - b0).astype(BF16)


def _peer_retrieve(y, g, w_q, keys, *, tm):
    n = y.shape[0]
    k1_shape = jax.ShapeDtypeStruct((PEER_HEADS, N_KEYS // SUBLANES, SUBLANES, n), F32)
    k2_shape = jax.ShapeDtypeStruct((PEER_HEADS, N_KEYS, n), BF16)
    k1_spec = pl.BlockSpec((PEER_HEADS, N_KEYS // SUBLANES, SUBLANES, tm), lambda i: (0, 0, 0, i))
    fac_spec = pl.BlockSpec((PEER_HEADS, N_KEYS, tm), lambda i: (0, 0, i))
    return pl.pallas_call(
        _peer_retrieve_kernel, grid=(n // tm,),
        in_specs=[pl.BlockSpec((tm, D_MODEL), lambda i: (i, 0)),
                  pl.BlockSpec((1, D_MODEL), lambda i: (0, 0)),
                  _resident((D_MODEL, D_MODEL), lambda i: (0, 0)),
                  pl.BlockSpec((PEER_HEADS, 2, N_KEYS, N_KEYS), lambda i: (0, 0, 0, 0))],
        out_specs=(pl.BlockSpec((D_MODEL, tm), lambda i: (0, i)),
                   k1_spec, k1_spec, fac_spec, fac_spec),
        out_shape=(jax.ShapeDtypeStruct((D_MODEL, n), BF16),
                   k1_shape, k1_shape, k2_shape, k2_shape),
        scratch_shapes=[pltpu.VMEM((N_KEYS, tm), F32), pltpu.VMEM((N_KEYS, tm), F32),
                        pltpu.VMEM((24, tm), F32),
                        pltpu.VMEM((24, tm), F32), pltpu.VMEM((72, tm), F32),
                        pltpu.VMEM((24, tm), F32), pltpu.VMEM((N_KEYS, tm), F32)],
        compiler_params=_cparams(1), name="peer_retrieve",
    )(y, g, w_q, keys)


def _peer_dense_kernel(*refs, eb, final_norm):
    if final_norm:
        (hbt_ref, y_ref, u_ref, vlo_ref, vhi_ref, cnt_ref, rr_ref, cntn_ref, rrn_ref, rk2_ref, p2_ref,
         gf_ref, o_ref, wa_scr, wb_scr, ga_scr, gb_scr, bc_scr) = refs
    else:
        (hbt_ref, y_ref, u_ref, vlo_ref, vhi_ref, cnt_ref, rr_ref, cntn_ref, rrn_ref, rk2_ref, p2_ref,
         o_ref, wa_scr, wb_scr, ga_scr, gb_scr, bc_scr) = refs
    j = pl.program_id(1)
    last = pl.num_programs(1) - 1
    n_sub = eb // N_KEYS
    tm = o_ref.shape[0]

    def gates(cnt_blk, rr_blk, half, dst_scr):
        for hh in range(PEER_HEADS):
            for i in range(n_sub):
                r = half * n_sub + i
                k = (hh * n_sub + i) * BF16_ROWS
                bc_scr[0, k:k + BF16_ROWS, :] = jnp.broadcast_to(
                    cnt_blk[hh, 0, r:r + 1, :].astype(BF16), (BF16_ROWS, tm))
                bc_scr[1, k:k + BF16_ROWS, :] = jnp.broadcast_to(
                    rr_blk[hh, 0, r:r + 1, :].astype(BF16), (BF16_ROWS, tm))
        for i in range(n_sub):
            for c in range(N_KEYS // BF16_ROWS):
                rows = slice(c * BF16_ROWS, (c + 1) * BF16_ROWS)
                gate = jnp.zeros((BF16_ROWS, tm), BF16)
                for hh in range(PEER_HEADS):
                    k = (hh * n_sub + i) * BF16_ROWS
                    cnt = bc_scr[0, k:k + BF16_ROWS, :]
                    rr = bc_scr[1, k:k + BF16_ROWS, :]
                    gate = gate + jnp.where(rk2_ref[hh, rows, :] < cnt, p2_ref[hh, rows, :] * rr,
                                            jnp.zeros((), BF16))
                dst_scr[i * N_KEYS + c * BF16_ROWS:i * N_KEYS + (c + 1) * BF16_ROWS, :] = gate

    def up(half, g_scr, dst_scr):
        a = _dot(u_ref[half * eb:(half + 1) * eb, :], hbt_ref[...])
        gelu2 = a * (1.0 + lax.erf(a * (1.0 / math.sqrt(2.0))))
        dst_scr[...] = gelu2.astype(BF16) * g_scr[...]

    @pl.when(j == 0)
    def _():
        o_ref[...] = y_ref[...]
        wb_scr[...] = jnp.zeros_like(wb_scr)
        gates(cnt_ref, rr_ref, 0, ga_scr)

    o_ref[...] += _dot_tn(wb_scr[...], vlo_ref[...])
    up(0, ga_scr, wa_scr)
    gates(cnt_ref, rr_ref, 1, gb_scr)

    @pl.when(j < last)
    def _():
        o_ref[...] += _dot_tn(wa_scr[...], vhi_ref[...])
        up(1, gb_scr, wb_scr)
        gates(cntn_ref, rrn_ref, 0, ga_scr)

    if final_norm:
        @pl.when(j == last)
        def _():
            o_ref[...] = _rmsnorm(o_ref[...], gf_ref[...])


def _peer_dense(hbt, y, u, v, cnt, rr, rk2, p2, *, layer, tm, eb, g_final=None):
    n = y.shape[0]
    n_blk = N_EXPERTS // eb
    n_pair = n_blk // 2
    n_sub = eb // N_KEYS
    assert cnt.shape == (PEER_HEADS, n_pair, 2 * n_sub, n), (cnt.shape, eb)
    tile = pl.BlockSpec((tm, D_MODEL), lambda i, j: (i, 0))
    fac_blk = pl.BlockSpec((PEER_HEADS, 1, 2 * n_sub, tm),
                           lambda i, j: (0, jnp.minimum(j, n_pair - 1), 0, i))
    fac_next = pl.BlockSpec((PEER_HEADS, 1, 2 * n_sub, tm),
                            lambda i, j: (0, jnp.minimum(j + 1, n_pair - 1), 0, i))
    in_specs = [
        _resident((D_MODEL, tm), lambda i, j: (0, i)),
        _resident((tm, D_MODEL), lambda i, j: (i, 0)),
        pl.BlockSpec((None, 2 * eb, D_MODEL), lambda i, j: (layer, jnp.minimum(j, n_pair - 1), 0)),
        pl.BlockSpec((None, eb, D_MODEL), lambda i, j: (layer, jnp.maximum(2 * j - 1, 0), 0)),
        pl.BlockSpec((None, eb, D_MODEL), lambda i, j: (layer, jnp.minimum(2 * j, n_blk - 1), 0)),
        fac_blk, fac_blk, fac_next, fac_next,
        _resident((PEER_HEADS, N_KEYS, tm), lambda i, j: (0, 0, i)),
        _resident((PEER_HEADS, N_KEYS, tm), lambda i, j: (0, 0, i)),
    ]
    args = [hbt, y, u, v, v, cnt, rr, cnt, rr, rk2, p2]
    if g_final is not None:
        in_specs.append(pl.BlockSpec((1, D_MODEL), lambda i, j: (0, 0)))
        args.append(g_final)
    return pl.pallas_call(
        functools.partial(_peer_dense_kernel, eb=eb, final_norm=g_final is not None),
        grid=(n // tm, n_pair + 1), in_specs=in_specs, out_specs=tile,
        out_shape=jax.ShapeDtypeStruct((n, D_MODEL), F32),
        scratch_shapes=[pltpu.VMEM((eb, tm), BF16)] * 4
        + [pltpu.VMEM((2, PEER_HEADS * n_sub * BF16_ROWS, tm), BF16)],
        compiler_params=_cparams(2), name="peer_dense",
    )(*args)


def _peer(y, g, w_q, keys, u, v, *, layer, tm_retrieve, tm, eb, g_final=None):
    hbt, cnt, rr, rk2, p2 = _peer_retrieve(y, g, w_q, keys, tm=tm_retrieve)
    return _peer_dense(hbt, y, u, v, cnt, rr, rk2, p2, layer=layer, tm=tm, eb=eb, g_final=g_final)


PEER_TM = 512
PEER_DENSE_TM = 512
PEER_EB = 512
SEQ_ROWS = 256
S5_ROWS = 512
S5_LONG_ROWS = 256
CAST_ROWS = 1024


def _row(v):
    return v.reshape(1, -1)


@jax.jit
def _step(x_prompt, x_sample, state_s5_re, state_s5_im, state_lru_h, state_lru_conv,
          norm_mix, norm_ffn, norm_final,
          s5_lam_re, s5_lam_im, s5_log_dt, s5_b_re, s5_b_im, s5_c_re, s5_c_im, s5_d, s5_w_glu, s5_b_glu,
          lru_w_in, lru_conv_w, lru_conv_b, lru_w_a, lru_b_a, lru_w_i, lru_b_i, lru_lam, lru_w_out,
          peer_w_q, peer_sub_keys, peer_u, peer_v):
    bsz, seq, d = x_prompt.shape
    dec_b, dec_t, _ = x_sample.shape
    xp = x_prompt.reshape(bsz * seq, d)
    xs = x_sample.reshape(dec_b * dec_t, d)
    n_s = dec_b * dec_t

    pow_re, pow_im, bb_re, bb_im = _s5_discretize(
        s5_lam_re[0], s5_lam_im[0], s5_log_dt[0], s5_b_re[0], s5_b_im[0])
    g_mix0 = _row(norm_mix[0])
    w_glu = s5_w_glu[0].astype(BF16)
    wb, wc = _s5_proj_weights(bb_re, bb_im, s5_c_re[0], s5_c_im[0])

    def glu(x, ymix):
        return _s5_glu(x, g_mix0, ymix, _row(s5_d[0]), w_glu, _row(s5_b_glu[0]), tm=PEER_TM)

    ymix, sre, sim = _s5_scan_long(
        x_prompt, g_mix0, wb, wc,
        pow_re[0].reshape(S5_KB, SUBLANES, LANES), pow_im[0].reshape(S5_KB, SUBLANES, LANES),
        rows=S5_LONG_ROWS)
    yp = glu(xp, ymix.reshape(bsz * seq, d))

    def final(s):
        return jnp.transpose(s, (1, 0, 2, 3)).reshape(1, bsz, S5_GROUPS, S5_STATE)
    s5_p = (final(sre), final(sim))

    assert dec_t in (1, 2, 4) and n_s % S5_ROWS == 0

    def by_block(s):
        return jnp.transpose(s.reshape(dec_b, S5_KB, S5_SW), (1, 0, 2))
    ymix, sre, sim = _s5_scan_short(
        xs, g_mix0, wb, wc, _s5_short_consts(pow_re, dec_t), _s5_short_consts(pow_im, dec_t),
        (by_block(state_s5_re[0]), by_block(state_s5_im[0])), rows=S5_ROWS, seg=dec_t)
    ys = glu(xs, ymix)

    def by_batch(s):
        return jnp.transpose(s, (1, 0, 2)).reshape(1, dec_b, S5_GROUPS, S5_STATE)
    s5_s = (by_batch(sre), by_batch(sim))

    u_bf = _cast_bf16(peer_u, rows=CAST_ROWS)
    v_bf = _cast_bf16(peer_v, rows=CAST_ROWS)

    def peer_layer(i, y, g_final=None):
        return _peer(y, _row(norm_ffn[i]), peer_w_q[i].astype(BF16), peer_sub_keys[i].astype(BF16),
                     u_bf, v_bf, layer=i,
                     tm_retrieve=PEER_TM, tm=min(PEER_DENSE_TM, y.shape[0]), eb=PEER_EB, g_final=g_final)
    yp = peer_layer(0, yp)
    ys = peer_layer(0, ys)

    g_mix1 = _row(norm_mix[1])
    w_in = lru_w_in[0].astype(BF16)
    w_out = lru_w_out[0].astype(BF16)
    gate_args = (lru_conv_w[0], _row(lru_conv_b[0]), lru_w_a[0].astype(BF16), _row(lru_b_a[0]),
                 lru_w_i[0].astype(BF16), _row(lru_b_i[0]), _row(lru_lam[0]))

    zp = _matmul(yp, w_in, tm=PEER_TM, tn=D_MODEL, g=g_mix1, name="lru_in_proj")
    gated_p, hst_p, xbst_p = _lru(zp, *gate_args, n_seq=bsz, rows=SEQ_ROWS)
    yp = _matmul(gated_p, w_out, tm=PEER_TM, tn=D_MODEL, res=yp, name="lru_out_proj")
    lru_h_p = hst_p[:, 0, :].reshape(1, bsz, d)
    lru_c_p = xbst_p[:, SUBLANES - (CONV_WIDTH - 1):, :].reshape(1, bsz, CONV_WIDTH - 1, d)

    buf = state_lru_conv[0]
    zero = jnp.zeros((dec_b, 1, d), F32)
    pcv = jnp.stack([
        jnp.concatenate([buf[:, 2:3], zero, zero, zero], axis=1),
        jnp.concatenate([buf[:, 1:3], zero, zero], axis=1),
        jnp.concatenate([buf[:, 0:3], zero], axis=1),
    ]).reshape(CONV_WIDTH - 1, n_s, d)
    h0 = jnp.repeat(state_lru_h[0], dec_t, axis=0)
    zs = _matmul(ys, w_in, tm=PEER_TM, tn=D_MODEL, g=g_mix1, name="lru_in_proj")
    gated_s, hst_s, xbst_s = _lru(zs, *gate_args, n_seq=n_s // SEQ_ROWS, rows=SEQ_ROWS, pcv=pcv, h0=h0)
    ys = _matmul(gated_s, w_out, tm=PEER_TM, tn=D_MODEL, res=ys, name="lru_out_proj")
    lru_h_s = hst_s.reshape(dec_b, dec_t, d)[:, dec_t - 1].reshape(1, dec_b, d)
    lru_c_s = xbst_s.reshape(dec_b, dec_t, d)[:, 1:].reshape(1, dec_b, CONV_WIDTH - 1, d)

    g_fin = _row(norm_final)
    yp = peer_layer(1, yp, g_final=g_fin)
    ys = peer_layer(1, ys, g_final=g_fin)

    return (yp.reshape(bsz, seq, d), ys.reshape(dec_b, dec_t, d),
            s5_p[0], s5_p[1], s5_s[0], s5_s[1],
            lru_h_p, lru_c_p, lru_h_s, lru_c_s)


def kernel(x_prompt, x_sample, state_s5_re, state_s5_im, state_lru_h, state_lru_conv, norm_mix, norm_ffn, norm_final, s5_lam_re, s5_lam_im, s5_log_dt, s5_b_re, s5_b_im, s5_c_re, s5_c_im, s5_d, s5_w_glu, s5_b_glu, lru_w_in, lru_conv_w, lru_conv_b, lru_w_a, lru_b_a, lru_w_i, lru_b_i, lru_lam, lru_w_out, peer_w_q, peer_sub_keys, peer_u, peer_v):
    return _step(x_prompt, x_sample, state_s5_re, state_s5_im, state_lru_h, state_lru_conv,
                 norm_mix, norm_ffn, norm_final,
                 s5_lam_re, s5_lam_im, s5_log_dt, s5_b_re, s5_b_im, s5_c_re, s5_c_im, s5_d, s5_w_glu, s5_b_glu,
                 lru_w_in, lru_conv_w, lru_conv_b, lru_w_a, lru_b_a, lru_w_i, lru_b_i, lru_lam, lru_w_out,
                 peer_w_q, peer_sub_keys, peer_u, peer_v)
```

```python
import functools
import math

import jax
import jax.numpy as jnp
from jax import lax
from jax.experimental import pallas as pl
from jax.experimental.pallas import tpu as pltpu

F32 = jnp.float32
BF16 = jnp.bfloat16

D_MODEL = 2048
RMS_EPS = 1e-6
S5_GROUP = 16
S5_GROUPS = 128
S5_STATE = 64
S5_KB = 8
S5_KBW = 256
S5_SW = 1024
LRU_HEADS = 8
LRU_BLOCK = 256
CONV_WIDTH = 4
LRU_C = 8.0
PEER_HEADS = 8
N_KEYS = 128
N_EXPERTS = N_KEYS * N_KEYS
TOPK = 16
SUBLANES = 8
LANES = 128
BF16_ROWS = 2 * SUBLANES
VMEM_LIMIT_BYTES = 56 * 1024 * 1024

NEG_INF = float("-inf")


def _cparams(n_axes, flags=None):
    return pltpu.CompilerParams(
        dimension_semantics=("arbitrary",) * n_axes,
        vmem_limit_bytes=VMEM_LIMIT_BYTES,
        flags=flags,
    )


def _resident(block_shape, index_map):
    return pl.BlockSpec(block_shape, index_map, pipeline_mode=pl.Buffered(1))


def _rmsnorm(x, g):
    ms = jnp.mean(x * x, axis=-1, keepdims=True)
    return x * lax.rsqrt(ms + RMS_EPS) * g


def _gelu(x):
    return 0.5 * x * (1.0 + lax.erf(x * (1.0 / math.sqrt(2.0))))


def _dot(a, b):
    return jnp.dot(a, b, preferred_element_type=F32)


def _dot_nt(a, b):
    return lax.dot_general(a, b, (((1,), (1,)), ((), ())), preferred_element_type=F32)


def _dot_tn(a, b):
    return lax.dot_general(a, b, (((0,), (0,)), ((), ())), preferred_element_type=F32)


def _cast_kernel(x_ref, o_ref):
    o_ref[...] = x_ref[...].astype(BF16)


def _cast_bf16(x, *, rows):
    layers, n, d = x.shape
    spec = pl.BlockSpec((None, rows, d), lambda l, i: (l, i, 0))
    return pl.pallas_call(
        _cast_kernel, grid=(layers, n // rows), in_specs=[spec], out_specs=spec,
        out_shape=jax.ShapeDtypeStruct((layers, n, d), BF16),
        compiler_params=_cparams(2), name="cast_bf16",
    )(x)


def _s5_discretize_kernel(lre_ref, lim_ref, ldt_ref, bre_ref, bim_ref,
                          pre_ref, pim_ref, bbre_ref, bbim_ref):
    lr = lre_ref[...]
    li = lim_ref[...]
    dt = jnp.exp(ldt_ref[...])
    mag = jnp.exp(lr * dt)
    ab_re = mag * jnp.cos(li * dt)
    ab_im = mag * jnp.sin(li * dt)
    nr, ni = ab_re - 1.0, ab_im
    den = lr * lr + li * li
    f_re = (nr * lr + ni * li) / den
    f_im = (ni * lr - nr * li) / den
    for c in range(S5_GROUP):
        br = bre_ref[c]
        bi = bim_ref[c]
        bbre_ref[c] = f_re * br - f_im * bi
        bbim_ref[c] = f_re * bi + f_im * br
    p_re, p_im = ab_re, ab_im
    for k in range(SUBLANES):
        pre_ref[k] = p_re
        pim_ref[k] = p_im
        p_re, p_im = p_re * ab_re - p_im * ab_im, p_re * ab_im + p_im * ab_re


def _s5_discretize(lam_re, lam_im, log_dt, b_re, b_im):
    g, p = S5_GROUPS, S5_STATE
    b_re_t = jnp.transpose(b_re, (2, 0, 1))
    b_im_t = jnp.transpose(b_im, (2, 0, 1))
    out_shape = (
        jax.ShapeDtypeStruct((SUBLANES, g, p), F32),
        jax.ShapeDtypeStruct((SUBLANES, g, p), F32),
        jax.ShapeDtypeStruct((S5_GROUP, g, p), F32),
        jax.ShapeDtypeStruct((S5_GROUP, g, p), F32),
    )
    return pl.pallas_call(_s5_discretize_kernel, out_shape=out_shape, name="s5_discretize")(
        lam_re, lam_im, log_dt.reshape(g, 1), b_re_t, b_im_t)


def _s5_proj_weights(bb_re, bb_im, c_re, c_im):
    eye = jnp.eye(S5_GROUP, dtype=F32)

    def in_proj(bb):
        bb = bb.reshape(S5_GROUP, S5_KB, S5_GROUP, S5_STATE)
        w = jnp.einsum("ckgp,gh->kgchp", bb, eye)
        return w.reshape(S5_KB, S5_KBW, S5_SW)

    def out_proj(c):
        c = c.reshape(S5_KB, S5_GROUP, S5_GROUP, S5_STATE)
        w = jnp.einsum("kgcp,gh->kgphc", c, eye)
        return w.reshape(S5_KB, S5_SW, S5_KBW)

    wb = jnp.concatenate([in_proj(bb_re), in_proj(bb_im)], axis=2).astype(BF16)
    wc = jnp.concatenate([out_proj(c_re), out_proj(-c_im)], axis=1).astype(BF16)
    return wb, wc


def _s5_short_consts(pw, seg):
    pos = jnp.arange(SUBLANES) % seg
    pw = pw.reshape(SUBLANES, S5_KB, S5_SW)
    steps = [pw[d - 1][:, None, :] * (pos >= d).astype(F32)[None, :, None] for d in (1, 2)]
    init = jnp.transpose(pw[pos], (1, 0, 2))
    return jnp.stack(steps + [init], axis=1)


def _s5_scan_short_kernel(x_ref, g_ref, wb_ref, wc_ref, cre_ref, cim_ref, h0re_ref, h0im_ref,
                          ymix_ref, sre_ref, sim_ref, hb_scr, bu_scr, *, seg):
    kb = pl.program_id(1)
    rows = x_ref.shape[0]
    n_sq = SUBLANES // seg
    pos8 = lax.broadcasted_iota(jnp.int32, (SUBLANES, S5_SW), 0)

    def per_row(h_ref, sq):
        def seq_row(q):
            return jnp.broadcast_to(h_ref[0, pl.ds(sq + q, 1), :], (SUBLANES, S5_SW))
        out = seq_row(n_sq - 1)
        for q in range(n_sq - 2, -1, -1):
            out = jnp.where(pos8 < (q + 1) * seg, seq_row(q), out)
        return out

    @pl.when(kb == 0)
    def _():
        hb = _rmsnorm(x_ref[...], g_ref[...]).astype(BF16)
        for j in range(S5_KB):
            hb_scr[j] = hb[:, j * S5_KBW:(j + 1) * S5_KBW]

    bu_scr[...] = _dot(hb_scr[kb], wb_ref[0])

    def body(r, carry):
        row = pl.multiple_of(r * SUBLANES, SUBLANES)
        re = bu_scr[pl.ds(row, SUBLANES), 0:S5_SW]
        im = bu_scr[pl.ds(row, SUBLANES), S5_SW:2 * S5_SW]
        for idx, d in enumerate((1, 2)):
            ar = cre_ref[0, idx]
            ai = cim_ref[0, idx]
            sr = pltpu.roll(re, d, 0)
            si = pltpu.roll(im, d, 0)
            re, im = re + ar * sr - ai * si, im + ar * si + ai * sr
        pr = cre_ref[0, 2]
        pi = cim_ref[0, 2]
        sq = r * n_sq
        cr = per_row(h0re_ref, sq)
        ci = per_row(h0im_ref, sq)
        re, im = re + pr * cr - pi * ci, im + pr * ci + pi * cr
        bu_scr[pl.ds(row, SUBLANES), 0:S5_SW] = re
        bu_scr[pl.ds(row, SUBLANES), S5_SW:2 * S5_SW] = im
        for q in range(n_sq):
            last = (q + 1) * seg - 1
            sre_ref[0, pl.ds(sq + q, 1), :] = re[last:last + 1]
            sim_ref[0, pl.ds(sq + q, 1), :] = im[last:last + 1]
        return carry

    lax.fori_loop(0, rows // SUBLANES, body, 0)
    ymix_ref[...] = _dot(bu_scr[...].astype(BF16), wc_ref[0])


def _s5_scan_short(x, g, wb, wc, cre, cim, h0, *, rows, seg):
    n = x.shape[0]
    c_spec = pl.BlockSpec((1, 3, SUBLANES, S5_SW), lambda i, k: (k, 0, 0, 0))
    st_spec = pl.BlockSpec((1, rows // seg, S5_SW), lambda i, k: (k, i, 0))
    st_shape = jax.ShapeDtypeStruct((S5_KB, n // seg, S5_SW), F32)
    return pl.pallas_call(
        functools.partial(_s5_scan_short_kernel, seg=seg), grid=(n // rows, S5_KB),
        in_specs=[pl.BlockSpec((rows, D_MODEL), lambda i, k: (i, 0)),
                  pl.BlockSpec((1, D_MODEL), lambda i, k: (0, 0)),
                  pl.BlockSpec((1, S5_KBW, 2 * S5_SW), lambda i, k: (k, 0, 0)),
                  pl.BlockSpec((1, 2 * S5_SW, S5_KBW), lambda i, k: (k, 0, 0)),
                  c_spec, c_spec, st_spec, st_spec],
        out_specs=(pl.BlockSpec((rows, S5_KBW), lambda i, k: (i, k)), st_spec, st_spec),
        out_shape=(jax.ShapeDtypeStruct((n, D_MODEL), F32), st_shape, st_shape),
        scratch_shapes=[pltpu.VMEM((S5_KB, rows, S5_KBW), BF16), pltpu.VMEM((rows, 2 * S5_SW), F32)],
        compiler_params=_cparams(2), name="s5_scan_short",
    )(x, g, wb, wc, cre, cim, h0[0], h0[1])


S5_CT = 2 * S5_SW // LANES


def _s5_scan_long_kernel(x_ref, g_ref, wb_ref, wc_ref, are_ref, aim_ref,
                         ymix_ref, sre_ref, sim_ref, hb_scr, t_scr, carry_scr):
    tc = pl.program_id(0)
    kb = pl.program_id(1)
    n_seq, rows = x_ref.shape[0], x_ref.shape[1]
    n_grp = rows // SUBLANES
    half = S5_CT // 2

    @pl.when(kb == 0)
    def _():
        for b in range(n_seq):
            hb = _rmsnorm(x_ref[b], g_ref[...]).astype(BF16)
            for j in range(S5_KB):
                hb_scr[j, b * rows:(b + 1) * rows, :] = hb[:, j * S5_KBW:(j + 1) * S5_KBW]

    @pl.when(jnp.logical_and(tc == 0, kb == 0))
    def _():
        carry_scr[...] = jnp.zeros_like(carry_scr)

    bu = _dot(hb_scr[kb], wb_ref[0])
    for b in range(n_seq):
        for g in range(n_grp):
            r0 = b * rows + g * SUBLANES
            for ct in range(S5_CT):
                t0 = (g * S5_CT + ct) * SUBLANES
                t_scr[b, t0:t0 + SUBLANES, :] = bu[r0:r0 + SUBLANES, ct * LANES:(ct + 1) * LANES]

    ar = are_ref[0]
    ai = aim_ref[0]

    def body(g, carry):
        carry = list(carry)
        for s in range(SUBLANES):
            for b in range(n_seq):
                i_re = pl.ds(g * (S5_CT * SUBLANES) + s, SUBLANES, stride=SUBLANES)
                i_im = pl.ds(g * (S5_CT * SUBLANES) + half * SUBLANES + s, SUBLANES, stride=SUBLANES)
                sr, si = carry[2 * b], carry[2 * b + 1]
                sr, si = (ar * sr - ai * si + t_scr[b, i_re, :],
                          ar * si + ai * sr + t_scr[b, i_im, :])
                t_scr[b, i_re, :] = sr
                t_scr[b, i_im, :] = si
                carry[2 * b], carry[2 * b + 1] = sr, si
        return tuple(carry)

    c0 = tuple(carry_scr[kb, i] for i in range(2 * n_seq))
    c = lax.fori_loop(0, n_grp, body, c0)
    for b in range(n_seq):
        carry_scr[kb, 2 * b] = c[2 * b]
        carry_scr[kb, 2 * b + 1] = c[2 * b + 1]
        sre_ref[kb, b] = c[2 * b]
        sim_ref[kb, b] = c[2 * b + 1]

    row_blocks = []
    for b in range(n_seq):
        for g in range(n_grp):
            tiles = [t_scr[b, (g * S5_CT + ct) * SUBLANES:(g * S5_CT + ct + 1) * SUBLANES, :]
                     for ct in range(S5_CT)]
            row_blocks.append(jnp.concatenate(tiles, axis=1))
    states = jnp.concatenate(row_blocks, axis=0).astype(BF16)
    ymix = _dot(states, wc_ref[0])
    for b in range(n_seq):
        ymix_ref[b] = ymix[b * rows:(b + 1) * rows]


def _s5_scan_long(x, g, wb, wc, a_re, a_im, *, rows):
    n_seq, t_len, _ = x.shape
    grid = (t_len // rows, S5_KB)
    st_shape = jax.ShapeDtypeStruct((S5_KB, n_seq, SUBLANES, LANES), F32)
    st_spec = pl.BlockSpec((S5_KB, n_seq, SUBLANES, LANES), lambda t, k: (0, 0, 0, 0))
    a_spec = pl.BlockSpec((1, SUBLANES, LANES), lambda t, k: (k, 0, 0))
    return pl.pallas_call(
        _s5_scan_long_kernel, grid=grid,
        in_specs=[pl.BlockSpec((n_seq, rows, D_MODEL), lambda t, k: (0, t, 0)),
                  pl.BlockSpec((1, D_MODEL), lambda t, k: (0, 0)),
                  pl.BlockSpec((1, S5_KBW, 2 * S5_SW), lambda t, k: (k, 0, 0)),
                  pl.BlockSpec((1, 2 * S5_SW, S5_KBW), lambda t, k: (k, 0, 0)),
                  a_spec, a_spec],
        out_specs=(pl.BlockSpec((n_seq, rows, S5_KBW), lambda t, k: (0, t, k)), st_spec, st_spec),
        out_shape=(jax.ShapeDtypeStruct((n_seq, t_len, D_MODEL), F32), st_shape, st_shape),
        scratch_shapes=[pltpu.VMEM((S5_KB, n_seq * rows, S5_KBW), BF16),
                        pltpu.VMEM((n_seq, rows * S5_CT, LANES), F32),
                        pltpu.VMEM((S5_KB, 2 * n_seq, SUBLANES, LANES), F32)],
        compiler_params=_cparams(2), name="s5_scan_long",
    )(x, g, wb, wc, a_re, a_im)


def _s5_glu_kernel(x_ref, g_ref, ymix_ref, d_ref, w_ref, b_ref, o_ref):
    x = x_ref[...]
    h = _rmsnorm(x, g_ref[...])
    y = _gelu(ymix_ref[...] + d_ref[...] * h)
    z = _dot(y.astype(BF16), w_ref[...]) + b_ref[...]
    o_ref[...] = x + y * jax.nn.sigmoid(z)


def _s5_glu(x, g, ymix, d_skip, w_glu, b_glu, *, tm):
    n = x.shape[0]
    tile = pl.BlockSpec((tm, D_MODEL), lambda i: (i, 0))
    vec = pl.BlockSpec((1, D_MODEL), lambda i: (0, 0))
    return pl.pallas_call(
        _s5_glu_kernel, grid=(n // tm,),
        in_specs=[tile, vec, tile, vec, _resident((D_MODEL, D_MODEL), lambda i: (0, 0)), vec],
        out_specs=tile, out_shape=jax.ShapeDtypeStruct((n, D_MODEL), F32),
        compiler_params=_cparams(1), name="s5_glu",
    )(x, g, ymix, d_skip, w_glu, b_glu)


def _matmul_kernel(*refs, norm, residual):
    refs = list(refs)
    x_ref = refs.pop(0)
    g_ref = refs.pop(0) if norm else None
    w_ref = refs.pop(0)
    r_ref = refs.pop(0) if residual else None
    o_ref = refs.pop(0)
    x = x_ref[...]
    if norm:
        x = _rmsnorm(x, g_ref[...]).astype(BF16)
    acc = _dot(x, w_ref[...])
    if residual:
        acc = acc + r_ref[...]
    o_ref[...] = acc


def _matmul(x, w, *, tm, tn, g=None, res=None, name):
    n, k = x.shape
    n_out = w.shape[1]
    grid = (n_out // tn, n // tm)
    in_specs = [pl.BlockSpec((tm, k), lambda j, i: (i, 0))]
    args = [x]
    if g is not None:
        in_specs.append(pl.BlockSpec((1, k), lambda j, i: (0, 0)))
        args.append(g)
    in_specs.append(pl.BlockSpec((k, tn), lambda j, i: (0, j)))
    args.append(w)
    if res is not None:
        in_specs.append(pl.BlockSpec((tm, tn), lambda j, i: (i, j)))
        args.append(res)
    return pl.pallas_call(
        functools.partial(_matmul_kernel, norm=g is not None, residual=res is not None),
        grid=grid, in_specs=in_specs,
        out_specs=pl.BlockSpec((tm, tn), lambda j, i: (i, j)),
        out_shape=jax.ShapeDtypeStruct((n, n_out), F32),
        compiler_params=_cparams(2), name=name,
    )(*args)


def _lru_kernel(*refs, seg4):
    if seg4:
        (z_ref, cw_ref, cb_ref, wa_ref, ba_ref, wi_ref, bi_ref, lam_ref, pcv_ref, h0_ref,
         gated_ref, hst_ref, xbst_ref, a_scr, b_scr) = refs
    else:
        (z_ref, cw_ref, cb_ref, wa_ref, ba_ref, wi_ref, bi_ref, lam_ref,
         gated_ref, hst_ref, xbst_ref, a_scr, b_scr, prev_scr, carry_scr) = refs
    tc = pl.program_id(1)
    rows = z_ref.shape[0]
    d = D_MODEL
    gate = z_ref[:, 0:d]
    xb = z_ref[:, d:2 * d]

    row8 = lax.broadcasted_iota(jnp.int32, (SUBLANES, d), 0)
    if seg4:
        t_full = lax.broadcasted_iota(jnp.int32, (rows, d), 0) % CONV_WIDTH
        xc = cb_ref[...] + cw_ref[3:4, :] * xb
        for s in range(1, CONV_WIDTH):
            shifted = jnp.where(t_full >= s, pltpu.roll(xb, s, 0), pcv_ref[s - 1])
            xc = xc + cw_ref[3 - s:4 - s, :] * shifted
        xbst_ref[...] = xb
        rseg = row8 % CONV_WIDTH
    else:
        @pl.when(tc == 0)
        def _():
            prev_scr[...] = jnp.zeros_like(prev_scr)
            carry_scr[...] = jnp.zeros_like(carry_scr)
        xcat = jnp.concatenate([prev_scr[...], xb], axis=0)
        xc = cb_ref[...] + cw_ref[3:4, :] * xb
        for s in range(1, CONV_WIDTH):
            xc = xc + cw_ref[3 - s:4 - s, :] * pltpu.roll(xcat, s, 0)[SUBLANES:]
        tail = xb[rows - SUBLANES:]
        prev_scr[...] = tail
        xbst_ref[0] = tail
        rseg = row8

    r_parts, i_parts = [], []
    for hh in range(LRU_HEADS):
        xh = xc[:, hh * LRU_BLOCK:(hh + 1) * LRU_BLOCK].astype(BF16)
        r_parts.append(_dot(xh, wa_ref[hh]))
        i_parts.append(_dot(xh, wi_ref[hh]))
    r = jax.nn.sigmoid(jnp.concatenate(r_parts, axis=1) + ba_ref[...])
    ig = jax.nn.sigmoid(jnp.concatenate(i_parts, axis=1) + bi_ref[...])
    log_a = -LRU_C * r * jax.nn.softplus(-lam_ref[...])
    a = jnp.exp(log_a)
    a_scr[...] = a
    b_scr[...] = jnp.sqrt(-jnp.tanh(log_a) * (a * a + 1.0)) * (ig * xc)

    masks = [rseg >= s for s in (1, 2, 4)]

    def body(g, carry):
        row = pl.multiple_of(g * SUBLANES, SUBLANES)
        av = a_scr[pl.ds(row, SUBLANES), :]
        bv = b_scr[pl.ds(row, SUBLANES), :]
        for m, s in zip(masks, (1, 2, 4)):
            if seg4 and s == 4:
                continue
            a_sh = jnp.where(m, pltpu.roll(av, s, 0), 1.0)
            b_sh = jnp.where(m, pltpu.roll(bv, s, 0), 0.0)
            bv = bv + av * b_sh
            av = av * a_sh
        c = h0_ref[pl.ds(row, SUBLANES), :] if seg4 else carry
        hv = bv + av * c
        b_scr[pl.ds(row, SUBLANES), :] = hv
        if seg4:
            return carry
        return jnp.broadcast_to(hv[SUBLANES - 1:SUBLANES], (SUBLANES, d))

    if seg4:
        lax.fori_loop(0, rows // SUBLANES, body, 0)
        hst_ref[...] = b_scr[...]
    else:
        c = lax.fori_loop(0, rows // SUBLANES, body, carry_scr[...])
        carry_scr[...] = c
        hst_ref[0] = c
    gated_ref[...] = (b_scr[...] * _gelu(gate)).astype(BF16)


def _lru(z, conv_w, conv_b, w_a, b_a, w_i, b_i, lam, *, n_seq, rows, pcv=None, h0=None):
    n = z.shape[0]
    seg4 = h0 is not None
    d = D_MODEL
    n_chunk = n // (n_seq * rows)

    def row_map(s, t):
        return (s * n_chunk + t, 0)

    vec = pl.BlockSpec((1, d), lambda s, t: (0, 0))
    gate_w = pl.BlockSpec((LRU_HEADS, LRU_BLOCK, LRU_BLOCK), lambda s, t: (0, 0, 0))
    in_specs = [pl.BlockSpec((rows, 2 * d), row_map),
                pl.BlockSpec((CONV_WIDTH, d), lambda s, t: (0, 0)), vec,
                gate_w, vec, gate_w, vec, vec]
    args = [z, conv_w, conv_b, w_a, b_a, w_i, b_i, lam]
    scratch = [pltpu.VMEM((rows, d), F32), pltpu.VMEM((rows, d), F32)]
    tile = pl.BlockSpec((rows, d), row_map)
    if seg4:
        in_specs += [pl.BlockSpec((CONV_WIDTH - 1, rows, d), lambda s, t: (0, s * n_chunk + t, 0)), tile]
        args += [pcv, h0]
        st_shape = jax.ShapeDtypeStruct((n, d), F32)
        st_spec = tile
    else:
        scratch += [pltpu.VMEM((SUBLANES, d), F32), pltpu.VMEM((SUBLANES, d), F32)]
        st_shape = jax.ShapeDtypeStruct((n_seq, SUBLANES, d), F32)
        st_spec = pl.BlockSpec((1, SUBLANES, d), lambda s, t: (s, 0, 0))
    return pl.pallas_call(
        functools.partial(_lru_kernel, seg4=seg4),
        grid=(n_seq, n_chunk), in_specs=in_specs,
        out_specs=(tile, st_spec, st_spec),
        out_shape=(jax.ShapeDtypeStruct((n, d), BF16), st_shape, st_shape),
        scratch_shapes=scratch, compiler_params=_cparams(2),
        name="lru_seg4" if seg4 else "lru",
    )(*args)


NO_RANK = 31.0


def _top_rows(vals_scr, top_scr, n_rows):
    def body(r, carry):
        v = vals_scr[0:n_rows]
        m = jnp.max(v, axis=0, keepdims=True)
        top_scr[pl.ds(r, 1), :] = m
        vals_scr[0:n_rows] = jnp.where(v == m, NEG_INF, v)
        return carry
    lax.fori_loop(0, TOPK, body, 0)


def _top_rows_pair(va_scr, ta_scr, vb_scr, tb_scr, rank_scr):
    rank_scr[...] = jnp.full(rank_scr.shape, NO_RANK, F32)

    def body(r, carry):
        va = va_scr[...]
        vb = vb_scr[...]
        ma = jnp.max(va, axis=0, keepdims=True)
        mb = jnp.max(vb, axis=0, keepdims=True)
        ta_scr[pl.ds(r, 1), :] = ma
        tb_scr[pl.ds(r, 1), :] = mb
        hit_b = vb == mb
        rank_scr[...] = jnp.where(hit_b, lax.convert_element_type(r, F32), rank_scr[...])
        va_scr[...] = jnp.where(va == ma, NEG_INF, va)
        vb_scr[...] = jnp.where(hit_b, NEG_INF, vb)
        return carry
    lax.fori_loop(0, TOPK, body, 0)


def _peer_retrieve_kernel(y_ref, g_ref, wq_ref, keys_ref,
                          hbt_ref, cnt_ref, rr_ref, rk2_ref, p2_ref,
                          vals_scr, vals2_scr, ta_scr, tb_scr, cand_scr, top_scr, rank_scr):
    tm = y_ref.shape[0]
    h = _rmsnorm(y_ref[...], g_ref[...])
    hb = h.astype(BF16)
    hbt_ref[...] = jnp.transpose(hb)
    qb = _dot(hb, wq_ref[...]).astype(BF16)
    row8 = lax.broadcasted_iota(jnp.int32, (SUBLANES, tm), 0)
    for hh in range(PEER_HEADS):
        c0 = hh * 2 * N_KEYS
        s1 = _dot_nt(keys_ref[hh, 0], qb[:, c0:c0 + N_KEYS])
        s2 = _dot_nt(keys_ref[hh, 1], qb[:, c0 + N_KEYS:c0 + 2 * N_KEYS])
        vals_scr[...] = s1
        vals2_scr[...] = s2
        _top_rows_pair(vals_scr, ta_scr, vals2_scr, tb_scr, rank_scr)
        a = ta_scr[0:TOPK]
        b = tb_scr[0:TOPK]
        a0, b0 = a[0:1], b[0:1]
        a_lo, b_lo = a[0:SUBLANES], b[0:SUBLANES]
        cand_scr[0:16] = a0 + b
        cand_scr[16:24] = a[1:2] + b_lo
        cand_scr[24:32] = jnp.where(row8 < 5, a[2:3] + b_lo, NEG_INF)
        cand_scr[32:40] = jnp.where(row8 < 4, a[3:4] + b_lo, NEG_INF)
        cand_scr[40:48] = jnp.where(row8 < 3, a[4:5] + b_lo, NEG_INF)
        cand_scr[48:56] = a[SUBLANES:2 * SUBLANES] + b0
        cand_scr[56:64] = jnp.where(row8 >= 5, a_lo + b0, NEG_INF)
        cand_scr[64:72] = jnp.where(row8 >= 5, a_lo + b[1:2], NEG_INF)
        cand = cand_scr[...]
        vals_scr[0:72] = cand
        _top_rows(vals_scr, top_scr, 72)
        tau = top_scr[TOPK - 1:TOPK]
        z = jnp.sum(jnp.where(cand >= tau, jnp.exp(cand - (a0 + b0)), 0.0), axis=0, keepdims=True)
        cnt = jnp.zeros_like(s1)
        for r in range(TOPK):
            cnt = cnt + jnp.where(s1 + b[r:r + 1] >= tau, 1.0, 0.0)
        rr = 0.5 * jnp.exp(s1 - a0 - jnp.log(z))
        for p in range(N_KEYS // SUBLANES):
            cnt_ref[hh, p] = cnt[p * SUBLANES:(p + 1) * SUBLANES]
            rr_ref[hh, p] = rr[p * SUBLANES:(p + 1) * SUBLANES]
        rk2_ref[hh] = rank_scr[...].astype(BF16)
        p2_ref[hh] = jnp.exp(s2 - b0).astype(BF16)


def _peer_retrieve(y, g, w_q, keys, *, tm):
    n = y.shape[0]
    k1_shape = jax.ShapeDtypeStruct((PEER_HEADS, N_KEYS // SUBLANES, SUBLANES, n), F32)
    k2_shape = jax.ShapeDtypeStruct((PEER_HEADS, N_KEYS, n), BF16)
    k1_spec = pl.BlockSpec((PEER_HEADS, N_KEYS // SUBLANES, SUBLANES, tm), lambda i: (0, 0, 0, i))
    fac_spec = pl.BlockSpec((PEER_HEADS, N_KEYS, tm), lambda i: (0, 0, i))
    return pl.pallas_call(
        _peer_retrieve_kernel, grid=(n // tm,),
        in_specs=[pl.BlockSpec((tm, D_MODEL), lambda i: (i, 0)),
                  pl.BlockSpec((1, D_MODEL), lambda i: (0, 0)),
                  _resident((D_MODEL, D_MODEL), lambda i: (0, 0)),
                  pl.BlockSpec((PEER_HEADS, 2, N_KEYS, N_KEYS), lambda i: (0, 0, 0, 0))],
        out_specs=(pl.BlockSpec((D_MODEL, tm), lambda i: (0, i)),
                   k1_spec, k1_spec, fac_spec, fac_spec),
        out_shape=(jax.ShapeDtypeStruct((D_MODEL, n), BF16),
                   k1_shape, k1_shape, k2_shape, k2_shape),
        scratch_shapes=[pltpu.VMEM((N_KEYS, tm), F32), pltpu.VMEM((N_KEYS, tm), F32),
                        pltpu.VMEM((24, tm), F32),
                        pltpu.VMEM((24, tm), F32), pltpu.VMEM((72, tm), F32),
                        pltpu.VMEM((24, tm), F32), pltpu.VMEM((N_KEYS, tm), F32)],
        compiler_params=_cparams(1), name="peer_retrieve",
    )(y, g, w_q, keys)


def _peer_dense_kernel(*refs, eb, final_norm):
    if final_norm:
        (hbt_ref, y_ref, u_ref, vlo_ref, vhi_ref, cnt_ref, rr_ref, cntn_ref, rrn_ref, rk2_ref, p2_ref,
         gf_ref, o_ref, wa_scr, wb_scr, ga_scr, gb_scr, bc0_scr, bc1_scr) = refs
    else:
        (hbt_ref, y_ref, u_ref, vlo_ref, vhi_ref, cnt_ref, rr_ref, cntn_ref, rrn_ref, rk2_ref, p2_ref,
         o_ref, wa_scr, wb_scr, ga_scr, gb_scr, bc0_scr, bc1_scr) = refs
    bc_scrs = (bc0_scr, bc1_scr)
    j = pl.program_id(1)
    last = pl.num_programs(1) - 1
    n_sub = eb // N_KEYS
    tm = o_ref.shape[0]

    def gates(cnt_blk, rr_blk, half, dst_scr):
        bc_scr = bc_scrs[half]
        for hh in range(PEER_HEADS):
            for i in range(n_sub):
                r = half * n_sub + i
                k = (hh * n_sub + i) * BF16_ROWS
                bc_scr[0, k:k + BF16_ROWS, :] = jnp.broadcast_to(
                    cnt_blk[hh, 0, r:r + 1, :].astype(BF16), (BF16_ROWS, tm))
                bc_scr[1, k:k + BF16_ROWS, :] = jnp.broadcast_to(
                    rr_blk[hh, 0, r:r + 1, :].astype(BF16), (BF16_ROWS, tm))
        for i in range(n_sub):
            for c in range(N_KEYS // BF16_ROWS):
                rows = slice(c * BF16_ROWS, (c + 1) * BF16_ROWS)
                gate = None
                for hh in range(PEER_HEADS):
                    k = (hh * n_sub + i) * BF16_ROWS
                    cnt = bc_scr[0, k:k + BF16_ROWS, :]
                    rr = bc_scr[1, k:k + BF16_ROWS, :]
                    term = jnp.where(rk2_ref[hh, rows, :] < cnt, p2_ref[hh, rows, :] * rr,
                                     jnp.zeros((), BF16))
                    gate = term if gate is None else gate + term
                dst_scr[i * N_KEYS + c * BF16_ROWS:i * N_KEYS + (c + 1) * BF16_ROWS, :] = gate

    def up(half, g_scr, dst_scr):
        a = _dot(u_ref[half * eb:(half + 1) * eb, :], hbt_ref[...])
        gelu2 = a * (1.0 + lax.erf(a * (1.0 / math.sqrt(2.0))))
        dst_scr[...] = gelu2.astype(BF16) * g_scr[...]

    @pl.when(j == 0)
    def _():
        o_ref[...] = y_ref[...]
        wb_scr[...] = jnp.zeros_like(wb_scr)
        gates(cnt_ref, rr_ref, 0, ga_scr)

    @pl.when(j < last)
    def _():
        o_ref[...] += _dot_tn(wb_scr[...], vlo_ref[...])
        up(0, ga_scr, wa_scr)
        gates(cnt_ref, rr_ref, 1, gb_scr)

    @pl.when(j == last)
    def _():
        o_ref[...] += _dot_tn(wb_scr[...], vlo_ref[...])

    @pl.when(j < last)
    def _():
        o_ref[...] += _dot_tn(wa_scr[...], vhi_ref[...])
        up(1, gb_scr, wb_scr)
        gates(cntn_ref, rrn_ref, 0, ga_scr)

    if final_norm:
        @pl.when(j == last)
        def _():
            o_ref[...] = _rmsnorm(o_ref[...], gf_ref[...])


def _peer_dense(hbt, y, u, v, cnt, rr, rk2, p2, *, layer, tm, eb, g_final=None):
    n = y.shape[0]
    n_blk = N_EXPERTS // eb
    n_pair = n_blk // 2
    n_sub = eb // N_KEYS
    assert cnt.shape == (PEER_HEADS, n_pair, 2 * n_sub, n), (cnt.shape, eb)
    tile = pl.BlockSpec((tm, D_MODEL), lambda i, j: (i, 0))
    fac_blk = pl.BlockSpec((PEER_HEADS, 1, 2 * n_sub, tm),
                           lambda i, j: (0, jnp.minimum(j, n_pair - 1), 0, i))
    fac_next = pl.BlockSpec((PEER_HEADS, 1, 2 * n_sub, tm),
                            lambda i, j: (0, jnp.minimum(j + 1, n_pair - 1), 0, i))
    in_specs = [
        _resident((D_MODEL, tm), lambda i, j: (0, i)),
        _resident((tm, D_MODEL), lambda i, j: (i, 0)),
        pl.BlockSpec((None, 2 * eb, D_MODEL), lambda i, j: (layer, jnp.minimum(j, n_pair - 1), 0)),
        pl.BlockSpec((None, eb, D_MODEL), lambda i, j: (layer, jnp.maximum(2 * j - 1, 0), 0)),
        pl.BlockSpec((None, eb, D_MODEL), lambda i, j: (layer, jnp.minimum(2 * j, n_blk - 1), 0)),
        fac_blk, fac_blk, fac_next, fac_next,
        _resident((PEER_HEADS, N_KEYS, tm), lambda i, j: (0, 0, i)),
        _resident((PEER_HEADS, N_KEYS, tm), lambda i, j: (0, 0, i)),
    ]
    args = [hbt, y, u, v, v, cnt, rr, cnt, rr, rk2, p2]
    if g_final is not None:
        in_specs.append(pl.BlockSpec((1, D_MODEL), lambda i, j: (0, 0)))
        args.append(g_final)
    return pl.pallas_call(
        functools.partial(_peer_dense_kernel, eb=eb, final_norm=g_final is not None),
        grid=(n // tm, n_pair + 1), in_specs=in_specs, out_specs=tile,
        out_shape=jax.ShapeDtypeStruct((n, D_MODEL), F32),
        scratch_shapes=[pltpu.VMEM((eb, tm), BF16)] * 4
        + [pltpu.VMEM((2, PEER_HEADS * n_sub * BF16_ROWS, tm), BF16)] * 2,
        compiler_params=_cparams(2), name="peer_dense",
    )(*args)


def _peer(y, g, w_q, keys, u, v, *, layer, tm_retrieve, tm, eb, g_final=None):
    hbt, cnt, rr, rk2, p2 = _peer_retrieve(y, g, w_q, keys, tm=tm_retrieve)
    return _peer_dense(hbt, y, u, v, cnt, rr, rk2, p2, layer=layer, tm=tm, eb=eb, g_final=g_final)


PEER_TM = 512
PEER_DENSE_TM = 512
PEER_EB = 512
SEQ_ROWS = 256
S5_ROWS = 512
S5_LONG_ROWS = 256
CAST_ROWS = 1024


def _row(v):
    return v.reshape(1, -1)


@jax.jit
def _step(x_prompt, x_sample, state_s5_re, state_s5_im, state_lru_h, state_lru_conv,
          norm_mix, norm_ffn, norm_final,
          s5_lam_re, s5_lam_im, s5_log_dt, s5_b_re, s5_b_im, s5_c_re, s5_c_im, s5_d, s5_w_glu, s5_b_glu,
          lru_w_in, lru_conv_w, lru_conv_b, lru_w_a, lru_b_a, lru_w_i, lru_b_i, lru_lam, lru_w_out,
          peer_w_q, peer_sub_keys, peer_u, peer_v):
    bsz, seq, d = x_prompt.shape
    dec_b, dec_t, _ = x_sample.shape
    xp = x_prompt.reshape(bsz * seq, d)
    xs = x_sample.reshape(dec_b * dec_t, d)
    n_s = dec_b * dec_t

    pow_re, pow_im, bb_re, bb_im = _s5_discretize(
        s5_lam_re[0], s5_lam_im[0], s5_log_dt[0], s5_b_re[0], s5_b_im[0])
    g_mix0 = _row(norm_mix[0])
    w_glu = s5_w_glu[0].astype(BF16)
    wb, wc = _s5_proj_weights(bb_re, bb_im, s5_c_re[0], s5_c_im[0])

    def glu(x, ymix):
        return _s5_glu(x, g_mix0, ymix, _row(s5_d[0]), w_glu, _row(s5_b_glu[0]), tm=PEER_TM)

    ymix, sre, sim = _s5_scan_long(
        x_prompt, g_mix0, wb, wc,
        pow_re[0].reshape(S5_KB, SUBLANES, LANES), pow_im[0].reshape(S5_KB, SUBLANES, LANES),
        rows=S5_LONG_ROWS)
    yp = glu(xp, ymix.reshape(bsz * seq, d))

    def final(s):
        return jnp.transpose(s, (1, 0, 2, 3)).reshape(1, bsz, S5_GROUPS, S5_STATE)
    s5_p = (final(sre), final(sim))

    assert dec_t in (1, 2, 4) and n_s % S5_ROWS == 0

    def by_block(s):
        return jnp.transpose(s.reshape(dec_b, S5_KB, S5_SW), (1, 0, 2))
    ymix, sre, sim = _s5_scan_short(
        xs, g_mix0, wb, wc, _s5_short_consts(pow_re, dec_t), _s5_short_consts(pow_im, dec_t),
        (by_block(state_s5_re[0]), by_block(state_s5_im[0])), rows=S5_ROWS, seg=dec_t)
    ys = glu(xs, ymix)

    def by_batch(s):
        return jnp.transpose(s, (1, 0, 2)).reshape(1, dec_b, S5_GROUPS, S5_STATE)
    s5_s = (by_batch(sre), by_batch(sim))

    u_bf = _cast_bf16(peer_u, rows=CAST_ROWS)
    v_bf = _cast_bf16(peer_v, rows=CAST_ROWS)

    def peer_layer(i, y, g_final=None):
        return _peer(y, _row(norm_ffn[i]), peer_w_q[i].astype(BF16), peer_sub_keys[i].astype(BF16),
                     u_bf, v_bf, layer=i,
                     tm_retrieve=PEER_TM, tm=min(PEER_DENSE_TM, y.shape[0]), eb=PEER_EB, g_final=g_final)
    yp = peer_layer(0, yp)
    ys = peer_layer(0, ys)

    g_mix1 = _row(norm_mix[1])
    w_in = lru_w_in[0].astype(BF16)
    w_out = lru_w_out[0].astype(BF16)
    gate_args = (lru_conv_w[0], _row(lru_conv_b[0]), lru_w_a[0].astype(BF16), _row(lru_b_a[0]),
                 lru_w_i[0].astype(BF16), _row(lru_b_i[0]), _row(lru_lam[0]))

    zp = _matmul(yp, w_in, tm=PEER_TM, tn=D_MODEL, g=g_mix1, name="lru_in_proj")
    gated_p, hst_p, xbst_p = _lru(zp, *gate_args, n_seq=bsz, rows=SEQ_ROWS)
    yp = _matmul(gated_p, w_out, tm=PEER_TM, tn=D_MODEL, res=yp, name="lru_out_proj")
    lru_h_p = hst_p[:, 0, :].reshape(1, bsz, d)
    lru_c_p = xbst_p[:, SUBLANES - (CONV_WIDTH - 1):, :].reshape(1, bsz, CONV_WIDTH - 1, d)

    buf = state_lru_conv[0]
    zero = jnp.zeros((dec_b, 1, d), F32)
    pcv = jnp.stack([
        jnp.concatenate([buf[:, 2:3], zero, zero, zero], axis=1),
        jnp.concatenate([buf[:, 1:3], zero, zero], axis=1),
        jnp.concatenate([buf[:, 0:3], zero], axis=1),
    ]).reshape(CONV_WIDTH - 1, n_s, d)
    h0 = jnp.repeat(state_lru_h[0], dec_t, axis=0)
    zs = _matmul(ys, w_in, tm=PEER_TM, tn=D_MODEL, g=g_mix1, name="lru_in_proj")
    gated_s, hst_s, xbst_s = _lru(zs, *gate_args, n_seq=n_s // SEQ_ROWS, rows=SEQ_ROWS, pcv=pcv, h0=h0)
    ys = _matmul(gated_s, w_out, tm=PEER_TM, tn=D_MODEL, res=ys, name="lru_out_proj")
    lru_h_s = hst_s.reshape(dec_b, dec_t, d)[:, dec_t - 1].reshape(1, dec_b, d)
    lru_c_s = xbst_s.reshape(dec_b, dec_t, d)[:, 1:].reshape(1, dec_b, CONV_WIDTH - 1, d)

    g_fin = _row(norm_final)
    yp = peer_layer(1, yp, g_final=g_fin)
    ys = peer_layer(1, ys, g_final=g_fin)

    return (yp.reshape(bsz, seq, d), ys.reshape(dec_b, dec_t, d),
            s5_p[0], s5_p[1], s5_s[0], s5_s[1],
            lru_h_p, lru_c_p, lru_h_s, lru_c_s)


def kernel(x_prompt, x_sample, state_s5_re, state_s5_im, state_lru_h, state_lru_conv, norm_mix, norm_ffn, norm_final, s5_lam_re, s5_lam_im, s5_log_dt, s5_b_re, s5_b_im, s5_c_re, s5_c_im, s5_d, s5_w_glu, s5_b_glu, lru_w_in, lru_conv_w, lru_conv_b, lru_w_a, lru_b_a, lru_w_i, lru_b_i, lru_lam, lru_w_out, peer_w_q, peer_sub_keys, peer_u, peer_v):
    return _step(x_prompt, x_sample, state_s5_re, state_s5_im, state_lru_h, state_lru_conv,
                 norm_mix, norm_ffn, norm_final,
                 s5_lam_re, s5_lam_im, s5_log_dt, s5_b_re, s5_b_im, s5_c_re, s5_c_im, s5_d, s5_w_glu, s5_b_glu,
                 lru_w_in, lru_conv_w, lru_conv_b, lru_w_a, lru_b_a, lru_w_i, lru_b_i, lru_lam, lru_w_out,
                 peer_w_q, peer_sub_keys, peer_u, peer_v)
```

```python
import functools
import math

import jax
import jax.numpy as jnp
from jax import lax
from jax.experimental import pallas as pl
from jax.experimental.pallas import tpu as pltpu

F32 = jnp.float32
BF16 = jnp.bfloat16

D_MODEL = 2048
RMS_EPS = 1e-6
S5_GROUP = 16
S5_GROUPS = 128
S5_STATE = 64
S5_KB = 8
S5_KBW = 256
S5_SW = 1024
LRU_HEADS = 8
LRU_BLOCK = 256
CONV_WIDTH = 4
LRU_C = 8.0
PEER_HEADS = 8
N_KEYS = 128
N_EXPERTS = N_KEYS * N_KEYS
TOPK = 16
SUBLANES = 8
LANES = 128
BF16_ROWS = 2 * SUBLANES
VMEM_LIMIT_BYTES = 56 * 1024 * 1024

NEG_INF = float("-inf")


def _cparams(n_axes, flags=None):
    return pltpu.CompilerParams(
        dimension_semantics=("arbitrary",) * n_axes,
        vmem_limit_bytes=VMEM_LIMIT_BYTES,
        flags=flags,
    )


def _resident(block_shape, index_map):
    return pl.BlockSpec(block_shape, index_map, pipeline_mode=pl.Buffered(1))


def _rmsnorm(x, g):
    ms = jnp.mean(x * x, axis=-1, keepdims=True)
    return x * lax.rsqrt(ms + RMS_EPS) * g


def _gelu(x):
    return 0.5 * x * (1.0 + lax.erf(x * (1.0 / math.sqrt(2.0))))


def _dot(a, b):
    return jnp.dot(a, b, preferred_element_type=F32)


def _dot_nt(a, b):
    return lax.dot_general(a, b, (((1,), (1,)), ((), ())), preferred_element_type=F32)


def _dot_tn(a, b):
    return lax.dot_general(a, b, (((0,), (0,)), ((), ())), preferred_element_type=F32)


def _cast_kernel(x_ref, o_ref):
    o_ref[...] = x_ref[...].astype(BF16)


def _cast_bf16(x, *, rows):
    layers, n, d = x.shape
    spec = pl.BlockSpec((None, rows, d), lambda l, i: (l, i, 0))
    return pl.pallas_call(
        _cast_kernel, grid=(layers, n // rows), in_specs=[spec], out_specs=spec,
        out_shape=jax.ShapeDtypeStruct((layers, n, d), BF16),
        compiler_params=_cparams(2), name="cast_bf16",
    )(x)


def _s5_discretize_kernel(lre_ref, lim_ref, ldt_ref, bre_ref, bim_ref,
                          pre_ref, pim_ref, bbre_ref, bbim_ref):
    lr = lre_ref[...]
    li = lim_ref[...]
    dt = jnp.exp(ldt_ref[...])
    mag = jnp.exp(lr * dt)
    ab_re = mag * jnp.cos(li * dt)
    ab_im = mag * jnp.sin(li * dt)
    nr, ni = ab_re - 1.0, ab_im
    den = lr * lr + li * li
    f_re = (nr * lr + ni * li) / den
    f_im = (ni * lr - nr * li) / den
    for c in range(S5_GROUP):
        br = bre_ref[c]
        bi = bim_ref[c]
        bbre_ref[c] = f_re * br - f_im * bi
        bbim_ref[c] = f_re * bi + f_im * br
    p_re, p_im = ab_re, ab_im
    for k in range(SUBLANES):
        pre_ref[k] = p_re
        pim_ref[k] = p_im
        p_re, p_im = p_re * ab_re - p_im * ab_im, p_re * ab_im + p_im * ab_re


def _s5_discretize(lam_re, lam_im, log_dt, b_re, b_im):
    g, p = S5_GROUPS, S5_STATE
    b_re_t = jnp.transpose(b_re, (2, 0, 1))
    b_im_t = jnp.transpose(b_im, (2, 0, 1))
    out_shape = (
        jax.ShapeDtypeStruct((SUBLANES, g, p), F32),
        jax.ShapeDtypeStruct((SUBLANES, g, p), F32),
        jax.ShapeDtypeStruct((S5_GROUP, g, p), F32),
        jax.ShapeDtypeStruct((S5_GROUP, g, p), F32),
    )
    return pl.pallas_call(_s5_discretize_kernel, out_shape=out_shape, name="s5_discretize")(
        lam_re, lam_im, log_dt.reshape(g, 1), b_re_t, b_im_t)


def _s5_proj_weights(bb_re, bb_im, c_re, c_im):
    eye = jnp.eye(S5_GROUP, dtype=F32)

    def in_proj(bb):
        bb = bb.reshape(S5_GROUP, S5_KB, S5_GROUP, S5_STATE)
        w = jnp.einsum("ckgp,gh->kgchp", bb, eye)
        return w.reshape(S5_KB, S5_KBW, S5_SW)

    def out_proj(c):
        c = c.reshape(S5_KB, S5_GROUP, S5_GROUP, S5_STATE)
        w = jnp.einsum("kgcp,gh->kgphc", c, eye)
        return w.reshape(S5_KB, S5_SW, S5_KBW)

    wb = jnp.concatenate([in_proj(bb_re), in_proj(bb_im)], axis=2).astype(BF16)
    wc = jnp.concatenate([out_proj(c_re), out_proj(-c_im)], axis=1).astype(BF16)
    return wb, wc


def _s5_short_consts(pw, seg):
    pos = jnp.arange(SUBLANES) % seg
    pw = pw.reshape(SUBLANES, S5_KB, S5_SW)
    steps = [pw[d - 1][:, None, :] * (pos >= d).astype(F32)[None, :, None] for d in (1, 2)]
    init = jnp.transpose(pw[pos], (1, 0, 2))
    return jnp.stack(steps + [init], axis=1)


def _s5_scan_short_kernel(x_ref, g_ref, wb_ref, wc_ref, cre_ref, cim_ref, h0re_ref, h0im_ref,
                          ymix_ref, sre_ref, sim_ref, hb_scr, bu_scr, *, seg):
    kb = pl.program_id(1)
    rows = x_ref.shape[0]
    n_sq = SUBLANES // seg
    pos8 = lax.broadcasted_iota(jnp.int32, (SUBLANES, S5_SW), 0)

    def per_row(h_ref, sq):
        def seq_row(q):
            return jnp.broadcast_to(h_ref[0, pl.ds(sq + q, 1), :], (SUBLANES, S5_SW))
        out = seq_row(n_sq - 1)
        for q in range(n_sq - 2, -1, -1):
            out = jnp.where(pos8 < (q + 1) * seg, seq_row(q), out)
        return out

    @pl.when(kb == 0)
    def _():
        hb = _rmsnorm(x_ref[...], g_ref[...]).astype(BF16)
        for j in range(S5_KB):
            hb_scr[j] = hb[:, j * S5_KBW:(j + 1) * S5_KBW]

    bu_scr[...] = _dot(hb_scr[kb], wb_ref[0])

    def body(r, carry):
        row = pl.multiple_of(r * SUBLANES, SUBLANES)
        re = bu_scr[pl.ds(row, SUBLANES), 0:S5_SW]
        im = bu_scr[pl.ds(row, SUBLANES), S5_SW:2 * S5_SW]
        for idx, d in enumerate((1, 2)):
            ar = cre_ref[0, idx]
            ai = cim_ref[0, idx]
            sr = pltpu.roll(re, d, 0)
            si = pltpu.roll(im, d, 0)
            re, im = re + ar * sr - ai * si, im + ar * si + ai * sr
        pr = cre_ref[0, 2]
        pi = cim_ref[0, 2]
        sq = r * n_sq
        cr = per_row(h0re_ref, sq)
        ci = per_row(h0im_ref, sq)
        re, im = re + pr * cr - pi * ci, im + pr * ci + pi * cr
        bu_scr[pl.ds(row, SUBLANES), 0:S5_SW] = re
        bu_scr[pl.ds(row, SUBLANES), S5_SW:2 * S5_SW] = im
        for q in range(n_sq):
            last = (q + 1) * seg - 1
            sre_ref[0, pl.ds(sq + q, 1), :] = re[last:last + 1]
            sim_ref[0, pl.ds(sq + q, 1), :] = im[last:last + 1]
        return carry

    lax.fori_loop(0, rows // SUBLANES, body, 0)
    ymix_ref[...] = _dot(bu_scr[...].astype(BF16), wc_ref[0])


def _s5_scan_short(x, g, wb, wc, cre, cim, h0, *, rows, seg):
    n = x.shape[0]
    c_spec = pl.BlockSpec((1, 3, SUBLANES, S5_SW), lambda i, k: (k, 0, 0, 0))
    st_spec = pl.BlockSpec((1, rows // seg, S5_SW), lambda i, k: (k, i, 0))
    st_shape = jax.ShapeDtypeStruct((S5_KB, n // seg, S5_SW), F32)
    return pl.pallas_call(
        functools.partial(_s5_scan_short_kernel, seg=seg), grid=(n // rows, S5_KB),
        in_specs=[pl.BlockSpec((rows, D_MODEL), lambda i, k: (i, 0)),
                  pl.BlockSpec((1, D_MODEL), lambda i, k: (0, 0)),
                  pl.BlockSpec((1, S5_KBW, 2 * S5_SW), lambda i, k: (k, 0, 0)),
                  pl.BlockSpec((1, 2 * S5_SW, S5_KBW), lambda i, k: (k, 0, 0)),
                  c_spec, c_spec, st_spec, st_spec],
        out_specs=(pl.BlockSpec((rows, S5_KBW), lambda i, k: (i, k)), st_spec, st_spec),
        out_shape=(jax.ShapeDtypeStruct((n, D_MODEL), F32), st_shape, st_shape),
        scratch_shapes=[pltpu.VMEM((S5_KB, rows, S5_KBW), BF16), pltpu.VMEM((rows, 2 * S5_SW), F32)],
        compiler_params=_cparams(2), name="s5_scan_short",
    )(x, g, wb, wc, cre, cim, h0[0], h0[1])


S5_CT = 2 * S5_SW // LANES


def _s5_scan_long_kernel(x_ref, g_ref, wb_ref, wc_ref, are_ref, aim_ref,
                         ymix_ref, sre_ref, sim_ref, hb_scr, t_scr, carry_scr):
    tc = pl.program_id(0)
    kb = pl.program_id(1)
    n_seq, rows = x_ref.shape[0], x_ref.shape[1]
    n_grp = rows // SUBLANES
    half = S5_CT // 2

    @pl.when(kb == 0)
    def _():
        for b in range(n_seq):
            hb = _rmsnorm(x_ref[b], g_ref[...]).astype(BF16)
            for j in range(S5_KB):
                hb_scr[j, b * rows:(b + 1) * rows, :] = hb[:, j * S5_KBW:(j + 1) * S5_KBW]

    @pl.when(jnp.logical_and(tc == 0, kb == 0))
    def _():
        carry_scr[...] = jnp.zeros_like(carry_scr)

    bu = _dot(hb_scr[kb], wb_ref[0])
    for b in range(n_seq):
        for g in range(n_grp):
            r0 = b * rows + g * SUBLANES
            for ct in range(S5_CT):
                t0 = (g * S5_CT + ct) * SUBLANES
                t_scr[b, t0:t0 + SUBLANES, :] = bu[r0:r0 + SUBLANES, ct * LANES:(ct + 1) * LANES]

    ar = are_ref[0]
    ai = aim_ref[0]

    def body(g, carry):
        carry = list(carry)
        for s in range(SUBLANES):
            for b in range(n_seq):
                i_re = pl.ds(g * (S5_CT * SUBLANES) + s, SUBLANES, stride=SUBLANES)
                i_im = pl.ds(g * (S5_CT * SUBLANES) + half * SUBLANES + s, SUBLANES, stride=SUBLANES)
                sr, si = carry[2 * b], carry[2 * b + 1]
                sr, si = (ar * sr - ai * si + t_scr[b, i_re, :],
                          ar * si + ai * sr + t_scr[b, i_im, :])
                t_scr[b, i_re, :] = sr
                t_scr[b, i_im, :] = si
                carry[2 * b], carry[2 * b + 1] = sr, si
        return tuple(carry)

    c0 = tuple(carry_scr[kb, i] for i in range(2 * n_seq))
    c = lax.fori_loop(0, n_grp, body, c0)
    for b in range(n_seq):
        carry_scr[kb, 2 * b] = c[2 * b]
        carry_scr[kb, 2 * b + 1] = c[2 * b + 1]
        sre_ref[kb, b] = c[2 * b]
        sim_ref[kb, b] = c[2 * b + 1]

    row_blocks = []
    for b in range(n_seq):
        for g in range(n_grp):
            tiles = [t_scr[b, (g * S5_CT + ct) * SUBLANES:(g * S5_CT + ct + 1) * SUBLANES, :]
                     for ct in range(S5_CT)]
            row_blocks.append(jnp.concatenate(tiles, axis=1))
    states = jnp.concatenate(row_blocks, axis=0).astype(BF16)
    ymix = _dot(states, wc_ref[0])
    for b in range(n_seq):
        ymix_ref[b] = ymix[b * rows:(b + 1) * rows]


def _s5_scan_long(x, g, wb, wc, a_re, a_im, *, rows):
    n_seq, t_len, _ = x.shape
    grid = (t_len // rows, S5_KB)
    st_shape = jax.ShapeDtypeStruct((S5_KB, n_seq, SUBLANES, LANES), F32)
    st_spec = pl.BlockSpec((S5_KB, n_seq, SUBLANES, LANES), lambda t, k: (0, 0, 0, 0))
    a_spec = pl.BlockSpec((1, SUBLANES, LANES), lambda t, k: (k, 0, 0))
    return pl.pallas_call(
        _s5_scan_long_kernel, grid=grid,
        in_specs=[pl.BlockSpec((n_seq, rows, D_MODEL), lambda t, k: (0, t, 0)),
                  pl.BlockSpec((1, D_MODEL), lambda t, k: (0, 0)),
                  pl.BlockSpec((1, S5_KBW, 2 * S5_SW), lambda t, k: (k, 0, 0)),
                  pl.BlockSpec((1, 2 * S5_SW, S5_KBW), lambda t, k: (k, 0, 0)),
                  a_spec, a_spec],
        out_specs=(pl.BlockSpec((n_seq, rows, S5_KBW), lambda t, k: (0, t, k)), st_spec, st_spec),
        out_shape=(jax.ShapeDtypeStruct((n_seq, t_len, D_MODEL), F32), st_shape, st_shape),
        scratch_shapes=[pltpu.VMEM((S5_KB, n_seq * rows, S5_KBW), BF16),
                        pltpu.VMEM((n_seq, rows * S5_CT, LANES), F32),
                        pltpu.VMEM((S5_KB, 2 * n_seq, SUBLANES, LANES), F32)],
        compiler_params=_cparams(2), name="s5_scan_long",
    )(x, g, wb, wc, a_re, a_im)


def _s5_glu_kernel(x_ref, g_ref, ymix_ref, d_ref, w_ref, b_ref, o_ref):
    x = x_ref[...]
    h = _rmsnorm(x, g_ref[...])
    y = _gelu(ymix_ref[...] + d_ref[...] * h)
    z = _dot(y.astype(BF16), w_ref[...]) + b_ref[...]
    o_ref[...] = x + y * jax.nn.sigmoid(z)


def _s5_glu(x, g, ymix, d_skip, w_glu, b_glu, *, tm):
    n = x.shape[0]
    tile = pl.BlockSpec((tm, D_MODEL), lambda i: (i, 0))
    vec = pl.BlockSpec((1, D_MODEL), lambda i: (0, 0))
    return pl.pallas_call(
        _s5_glu_kernel, grid=(n // tm,),
        in_specs=[tile, vec, tile, vec, _resident((D_MODEL, D_MODEL), lambda i: (0, 0)), vec],
        out_specs=tile, out_shape=jax.ShapeDtypeStruct((n, D_MODEL), F32),
        compiler_params=_cparams(1), name="s5_glu",
    )(x, g, ymix, d_skip, w_glu, b_glu)


def _matmul_kernel(*refs, norm, residual):
    refs = list(refs)
    x_ref = refs.pop(0)
    g_ref = refs.pop(0) if norm else None
    w_ref = refs.pop(0)
    r_ref = refs.pop(0) if residual else None
    o_ref = refs.pop(0)
    x = x_ref[...]
    if norm:
        x = _rmsnorm(x, g_ref[...]).astype(BF16)
    acc = _dot(x, w_ref[...])
    if residual:
        acc = acc + r_ref[...]
    o_ref[...] = acc


def _matmul(x, w, *, tm, tn, g=None, res=None, name):
    n, k = x.shape
    n_out = w.shape[1]
    grid = (n_out // tn, n // tm)
    in_specs = [pl.BlockSpec((tm, k), lambda j, i: (i, 0))]
    args = [x]
    if g is not None:
        in_specs.append(pl.BlockSpec((1, k), lambda j, i: (0, 0)))
        args.append(g)
    in_specs.append(pl.BlockSpec((k, tn), lambda j, i: (0, j)))
    args.append(w)
    if res is not None:
        in_specs.append(pl.BlockSpec((tm, tn), lambda j, i: (i, j)))
        args.append(res)
    return pl.pallas_call(
        functools.partial(_matmul_kernel, norm=g is not None, residual=res is not None),
        grid=grid, in_specs=in_specs,
        out_specs=pl.BlockSpec((tm, tn), lambda j, i: (i, j)),
        out_shape=jax.ShapeDtypeStruct((n, n_out), F32),
        compiler_params=_cparams(2), name=name,
    )(*args)


def _lru_kernel(*refs, seg4):
    if seg4:
        (z_ref, cw_ref, cb_ref, wa_ref, ba_ref, wi_ref, bi_ref, lam_ref, pcv_ref, h0_ref,
         gated_ref, hst_ref, xbst_ref, a_scr, b_scr) = refs
    else:
        (z_ref, cw_ref, cb_ref, wa_ref, ba_ref, wi_ref, bi_ref, lam_ref,
         gated_ref, hst_ref, xbst_ref, a_scr, b_scr, prev_scr, carry_scr) = refs
    tc = pl.program_id(1)
    rows = z_ref.shape[0]
    d = D_MODEL
    gate = z_ref[:, 0:d]
    xb = z_ref[:, d:2 * d]

    row8 = lax.broadcasted_iota(jnp.int32, (SUBLANES, d), 0)
    if seg4:
        t_full = lax.broadcasted_iota(jnp.int32, (rows, d), 0) % CONV_WIDTH
        xc = cb_ref[...] + cw_ref[3:4, :] * xb
        for s in range(1, CONV_WIDTH):
            shifted = jnp.where(t_full >= s, pltpu.roll(xb, s, 0), pcv_ref[s - 1])
            xc = xc + cw_ref[3 - s:4 - s, :] * shifted
        xbst_ref[...] = xb
        rseg = row8 % CONV_WIDTH
    else:
        @pl.when(tc == 0)
        def _():
            prev_scr[...] = jnp.zeros_like(prev_scr)
            carry_scr[...] = jnp.zeros_like(carry_scr)
        xcat = jnp.concatenate([prev_scr[...], xb], axis=0)
        xc = cb_ref[...] + cw_ref[3:4, :] * xb
        for s in range(1, CONV_WIDTH):
            xc = xc + cw_ref[3 - s:4 - s, :] * pltpu.roll(xcat, s, 0)[SUBLANES:]
        tail = xb[rows - SUBLANES:]
        prev_scr[...] = tail
        xbst_ref[0] = tail
        rseg = row8

    r_parts, i_parts = [], []
    for hh in range(LRU_HEADS):
        xh = xc[:, hh * LRU_BLOCK:(hh + 1) * LRU_BLOCK].astype(BF16)
        r_parts.append(_dot(xh, wa_ref[hh]))
        i_parts.append(_dot(xh, wi_ref[hh]))
    r = jax.nn.sigmoid(jnp.concatenate(r_parts, axis=1) + ba_ref[...])
    ig = jax.nn.sigmoid(jnp.concatenate(i_parts, axis=1) + bi_ref[...])
    log_a = -LRU_C * r * jax.nn.softplus(-lam_ref[...])
    a = jnp.exp(log_a)
    a_scr[...] = a
    b_scr[...] = jnp.sqrt(-jnp.tanh(log_a) * (a * a + 1.0)) * (ig * xc)

    masks = [rseg >= s for s in (1, 2, 4)]

    def body(g, carry):
        row = pl.multiple_of(g * SUBLANES, SUBLANES)
        av = a_scr[pl.ds(row, SUBLANES), :]
        bv = b_scr[pl.ds(row, SUBLANES), :]
        for m, s in zip(masks, (1, 2, 4)):
            if seg4 and s == 4:
                continue
            a_sh = jnp.where(m, pltpu.roll(av, s, 0), 1.0)
            b_sh = jnp.where(m, pltpu.roll(bv, s, 0), 0.0)
            bv = bv + av * b_sh
            av = av * a_sh
        c = h0_ref[pl.ds(row, SUBLANES), :] if seg4 else carry
        hv = bv + av * c
        b_scr[pl.ds(row, SUBLANES), :] = hv
        if seg4:
            return carry
        return jnp.broadcast_to(hv[SUBLANES - 1:SUBLANES], (SUBLANES, d))

    if seg4:
        lax.fori_loop(0, rows // SUBLANES, body, 0)
        hst_ref[...] = b_scr[...]
    else:
        c = lax.fori_loop(0, rows // SUBLANES, body, carry_scr[...])
        carry_scr[...] = c
        hst_ref[0] = c
    gated_ref[...] = (b_scr[...] * _gelu(gate)).astype(BF16)


def _lru(z, conv_w, conv_b, w_a, b_a, w_i, b_i, lam, *, n_seq, rows, pcv=None, h0=None):
    n = z.shape[0]
    seg4 = h0 is not None
    d = D_MODEL
    n_chunk = n // (n_seq * rows)

    def row_map(s, t):
        return (s * n_chunk + t, 0)

    vec = pl.BlockSpec((1, d), lambda s, t: (0, 0))
    gate_w = pl.BlockSpec((LRU_HEADS, LRU_BLOCK, LRU_BLOCK), lambda s, t: (0, 0, 0))
    in_specs = [pl.BlockSpec((rows, 2 * d), row_map),
                pl.BlockSpec((CONV_WIDTH, d), lambda s, t: (0, 0)), vec,
                gate_w, vec, gate_w, vec, vec]
    args = [z, conv_w, conv_b, w_a, b_a, w_i, b_i, lam]
    scratch = [pltpu.VMEM((rows, d), F32), pltpu.VMEM((rows, d), F32)]
    tile = pl.BlockSpec((rows, d), row_map)
    if seg4:
        in_specs += [pl.BlockSpec((CONV_WIDTH - 1, rows, d), lambda s, t: (0, s * n_chunk + t, 0)), tile]
        args += [pcv, h0]
        st_shape = jax.ShapeDtypeStruct((n, d), F32)
        st_spec = tile
    else:
        scratch += [pltpu.VMEM((SUBLANES, d), F32), pltpu.VMEM((SUBLANES, d), F32)]
        st_shape = jax.ShapeDtypeStruct((n_seq, SUBLANES, d), F32)
        st_spec = pl.BlockSpec((1, SUBLANES, d), lambda s, t: (s, 0, 0))
    return pl.pallas_call(
        functools.partial(_lru_kernel, seg4=seg4),
        grid=(n_seq, n_chunk), in_specs=in_specs,
        out_specs=(tile, st_spec, st_spec),
        out_shape=(jax.ShapeDtypeStruct((n, d), BF16), st_shape, st_shape),
        scratch_shapes=scratch, compiler_params=_cparams(2),
        name="lru_seg4" if seg4 else "lru",
    )(*args)


NO_RANK = 31.0


def _top_rows(vals_scr, top_scr, n_rows):
    def body(r, carry):
        v = vals_scr[0:n_rows]
        m = jnp.max(v, axis=0, keepdims=True)
        top_scr[pl.ds(r, 1), :] = m
        vals_scr[0:n_rows] = jnp.where(v == m, NEG_INF, v)
        return carry
    lax.fori_loop(0, TOPK, body, 0)


def _top_rows_pair(va_scr, ta_scr, vb_scr, tb_scr, rank_scr):
    rank_scr[...] = jnp.full(rank_scr.shape, NO_RANK, F32)

    def body(r, carry):
        va = va_scr[...]
        vb = vb_scr[...]
        ma = jnp.max(va, axis=0, keepdims=True)
        mb = jnp.max(vb, axis=0, keepdims=True)
        ta_scr[pl.ds(r, 1), :] = ma
        tb_scr[pl.ds(r, 1), :] = mb
        hit_b = vb == mb
        rank_scr[...] = jnp.where(hit_b, lax.convert_element_type(r, F32), rank_scr[...])
        va_scr[...] = jnp.where(va == ma, NEG_INF, va)
        vb_scr[...] = jnp.where(hit_b, NEG_INF, vb)
        return carry
    lax.fori_loop(0, TOPK, body, 0)


def _peer_retrieve_kernel(y_ref, g_ref, wq_ref, keys_ref,
                          hbt_ref, cnt_ref, rr_ref, rk2_ref, p2_ref,
                          vals_scr, vals2_scr, ta_scr, tb_scr, cand_scr, top_scr, rank_scr):
    tm = y_ref.shape[0]
    h = _rmsnorm(y_ref[...], g_ref[...])
    hb = h.astype(BF16)
    hbt_ref[...] = jnp.transpose(hb)
    qb = _dot(hb, wq_ref[...]).astype(BF16)
    row8 = lax.broadcasted_iota(jnp.int32, (SUBLANES, tm), 0)
    for hh in range(PEER_HEADS):
        c0 = hh * 2 * N_KEYS
        s1 = _dot_nt(keys_ref[hh, 0], qb[:, c0:c0 + N_KEYS])
        s2 = _dot_nt(keys_ref[hh, 1], qb[:, c0 + N_KEYS:c0 + 2 * N_KEYS])
        vals_scr[...] = s1
        vals2_scr[...] = s2
        _top_rows_pair(vals_scr, ta_scr, vals2_scr, tb_scr, rank_scr)
        a = ta_scr[0:TOPK]
        b = tb_scr[0:TOPK]
        a0, b0 = a[0:1], b[0:1]
        a_lo, b_lo = a[0:SUBLANES], b[0:SUBLANES]
        cand_scr[0:16] = a0 + b
        cand_scr[16:24] = a[1:2] + b_lo
        cand_scr[24:32] = jnp.where(row8 < 5, a[2:3] + b_lo, NEG_INF)
        cand_scr[32:40] = jnp.where(row8 < 4, a[3:4] + b_lo, NEG_INF)
        cand_scr[40:48] = jnp.where(row8 < 3, a[4:5] + b_lo, NEG_INF)
        cand_scr[48:56] = a[SUBLANES:2 * SUBLANES] + b0
        cand_scr[56:64] = jnp.where(row8 >= 5, a_lo + b0, NEG_INF)
        cand_scr[64:72] = jnp.where(row8 >= 5, a_lo + b[1:2], NEG_INF)
        cand = cand_scr[...]
        vals_scr[0:72] = cand
        _top_rows(vals_scr, top_scr, 72)
        tau = top_scr[TOPK - 1:TOPK]
        z = jnp.sum(jnp.where(cand >= tau, jnp.exp(cand - (a0 + b0)), 0.0), axis=0, keepdims=True)
        cnt = jnp.zeros_like(s1)
        for r in range(TOPK):
            cnt = cnt + jnp.where(s1 + b[r:r + 1] >= tau, 1.0, 0.0)
        rr = 0.5 * jnp.exp(s1 - a0 - jnp.log(z))
        for p in range(N_KEYS // SUBLANES):
            cnt_ref[hh, p] = cnt[p * SUBLANES:(p + 1) * SUBLANES]
            rr_ref[hh, p] = rr[p * SUBLANES:(p + 1) * SUBLANES]
        rk2_ref[hh] = rank_scr[...].astype(BF16)
        p2_ref[hh] = jnp.exp(s2 - b0).astype(BF16)


def _peer_retrieve(y, g, w_q, keys, *, tm):
    n = y.shape[0]
    k1_shape = jax.ShapeDtypeStruct((PEER_HEADS, N_KEYS // SUBLANES, SUBLANES, n), F32)
    k2_shape = jax.ShapeDtypeStruct((PEER_HEADS, N_KEYS, n), BF16)
    k1_spec = pl.BlockSpec((PEER_HEADS, N_KEYS // SUBLANES, SUBLANES, tm), lambda i: (0, 0, 0, i))
    fac_spec = pl.BlockSpec((PEER_HEADS, N_KEYS, tm), lambda i: (0, 0, i))
    return pl.pallas_call(
        _peer_retrieve_kernel, grid=(n // tm,),
        in_specs=[pl.BlockSpec((tm, D_MODEL), lambda i: (i, 0)),
                  pl.BlockSpec((1, D_MODEL), lambda i: (0, 0)),
                  _resident((D_MODEL, D_MODEL), lambda i: (0, 0)),
                  pl.BlockSpec((PEER_HEADS, 2, N_KEYS, N_KEYS), lambda i: (0, 0, 0, 0))],
        out_specs=(pl.BlockSpec((D_MODEL, tm), lambda i: (0, i)),
                   k1_spec, k1_spec, fac_spec, fac_spec),
        out_shape=(jax.ShapeDtypeStruct((D_MODEL, n), BF16),
                   k1_shape, k1_shape, k2_shape, k2_shape),
        scratch_shapes=[pltpu.VMEM((N_KEYS, tm), F32), pltpu.VMEM((N_KEYS, tm), F32),
                        pltpu.VMEM((24, tm), F32),
                        pltpu.VMEM((24, tm), F32), pltpu.VMEM((72, tm), F32),
                        pltpu.VMEM((24, tm), F32), pltpu.VMEM((N_KEYS, tm), F32)],
        compiler_params=_cparams(1), name="peer_retrieve",
    )(y, g, w_q, keys)


def _peer_dense_kernel(*refs, eb, final_norm):
    if final_norm:
        (hbt_ref, y_ref, u_ref, vlo_ref, vhi_ref, cnt_ref, rr_ref, cntn_ref, rrn_ref, rk2_ref, p2_ref,
         gf_ref, o_ref, wa_scr, wb_scr, ga_scr, gb_scr, bc0_scr, bc1_scr) = refs
    else:
        (hbt_ref, y_ref, u_ref, vlo_ref, vhi_ref, cnt_ref, rr_ref, cntn_ref, rrn_ref, rk2_ref, p2_ref,
         o_ref, wa_scr, wb_scr, ga_scr, gb_scr, bc0_scr, bc1_scr) = refs
    bc_scrs = (bc0_scr, bc1_scr)
    j = pl.program_id(1)
    last = pl.num_programs(1) - 1
    n_sub = eb // N_KEYS
    tm = o_ref.shape[0]

    tw = tm // DENSE_TOKEN_SPLIT
    tok = [slice(t * tw, (t + 1) * tw) for t in range(DENSE_TOKEN_SPLIT)]

    def gates(cnt_blk, rr_blk, half, dst_scr, ts):
        bc_scr = bc_scrs[half]
        for hh in range(PEER_HEADS):
            for i in range(n_sub):
                r = half * n_sub + i
                k = (hh * n_sub + i) * BF16_ROWS
                bc_scr[0, k:k + BF16_ROWS, ts] = jnp.broadcast_to(
                    cnt_blk[hh, 0, r:r + 1, ts].astype(BF16), (BF16_ROWS, tw))
                bc_scr[1, k:k + BF16_ROWS, ts] = jnp.broadcast_to(
                    rr_blk[hh, 0, r:r + 1, ts].astype(BF16), (BF16_ROWS, tw))
        for i in range(n_sub):
            for c in range(N_KEYS // BF16_ROWS):
                rows = slice(c * BF16_ROWS, (c + 1) * BF16_ROWS)
                gate = None
                for hh in range(PEER_HEADS):
                    k = (hh * n_sub + i) * BF16_ROWS
                    cnt = bc_scr[0, k:k + BF16_ROWS, ts]
                    rr = bc_scr[1, k:k + BF16_ROWS, ts]
                    term = jnp.where(rk2_ref[hh, rows, ts] < cnt, p2_ref[hh, rows, ts] * rr,
                                     jnp.zeros((), BF16))
                    gate = term if gate is None else gate + term
                dst_scr[i * N_KEYS + c * BF16_ROWS:i * N_KEYS + (c + 1) * BF16_ROWS, ts] = gate

    def up(half, g_scr, dst_scr, ts):
        a = _dot(u_ref[half * eb:(half + 1) * eb, :], hbt_ref[:, ts])
        gelu2 = a * (1.0 + lax.erf(a * (1.0 / math.sqrt(2.0))))
        dst_scr[:, ts] = gelu2.astype(BF16) * g_scr[:, ts]

    def down(w_scr, v_ref, ts):
        o_ref[ts, :] += _dot_tn(w_scr[:, ts], v_ref[...])

    @pl.when(j == 0)
    def _():
        o_ref[...] = y_ref[...]
        wb_scr[...] = jnp.zeros_like(wb_scr)
        for ts in tok:
            gates(cnt_ref, rr_ref, 0, ga_scr, ts)

    @pl.when(j < last)
    def _():
        for ts in tok:
            down(wb_scr, vlo_ref, ts)
            up(0, ga_scr, wa_scr, ts)
            gates(cnt_ref, rr_ref, 1, gb_scr, ts)

    @pl.when(j == last)
    def _():
        for ts in tok:
            down(wb_scr, vlo_ref, ts)

    @pl.when(j < last)
    def _():
        for ts in tok:
            down(wa_scr, vhi_ref, ts)
            up(1, gb_scr, wb_scr, ts)
            gates(cntn_ref, rrn_ref, 0, ga_scr, ts)

    if final_norm:
        @pl.when(j == last)
        def _():
            o_ref[...] = _rmsnorm(o_ref[...], gf_ref[...])


def _peer_dense(hbt, y, u, v, cnt, rr, rk2, p2, *, layer, tm, eb, g_final=None):
    n = y.shape[0]
    n_blk = N_EXPERTS // eb
    n_pair = n_blk // 2
    n_sub = eb // N_KEYS
    assert cnt.shape == (PEER_HEADS, n_pair, 2 * n_sub, n), (cnt.shape, eb)
    tile = pl.BlockSpec((tm, D_MODEL), lambda i, j: (i, 0))
    fac_blk = pl.BlockSpec((PEER_HEADS, 1, 2 * n_sub, tm),
                           lambda i, j: (0, jnp.minimum(j, n_pair - 1), 0, i))
    fac_next = pl.BlockSpec((PEER_HEADS, 1, 2 * n_sub, tm),
                            lambda i, j: (0, jnp.minimum(j + 1, n_pair - 1), 0, i))
    in_specs = [
        _resident((D_MODEL, tm), lambda i, j: (0, i)),
        _resident((tm, D_MODEL), lambda i, j: (i, 0)),
        pl.BlockSpec((None, 2 * eb, D_MODEL), lambda i, j: (layer, jnp.minimum(j, n_pair - 1), 0)),
        pl.BlockSpec((None, eb, D_MODEL), lambda i, j: (layer, jnp.maximum(2 * j - 1, 0), 0)),
        pl.BlockSpec((None, eb, D_MODEL), lambda i, j: (layer, jnp.minimum(2 * j, n_blk - 1), 0)),
        fac_blk, fac_blk, fac_next, fac_next,
        _resident((PEER_HEADS, N_KEYS, tm), lambda i, j: (0, 0, i)),
        _resident((PEER_HEADS, N_KEYS, tm), lambda i, j: (0, 0, i)),
    ]
    args = [hbt, y, u, v, v, cnt, rr, cnt, rr, rk2, p2]
    if g_final is not None:
        in_specs.append(pl.BlockSpec((1, D_MODEL), lambda i, j: (0, 0)))
        args.append(g_final)
    return pl.pallas_call(
        functools.partial(_peer_dense_kernel, eb=eb, final_norm=g_final is not None),
        grid=(n // tm, n_pair + 1), in_specs=in_specs, out_specs=tile,
        out_shape=jax.ShapeDtypeStruct((n, D_MODEL), F32),
        scratch_shapes=[pltpu.VMEM((eb, tm), BF16)] * 4
        + [pltpu.VMEM((2, PEER_HEADS * n_sub * BF16_ROWS, tm), BF16)] * 2,
        compiler_params=_cparams(2), name="peer_dense",
    )(*args)


def _peer(y, g, w_q, keys, u, v, *, layer, tm_retrieve, tm, eb, g_final=None):
    hbt, cnt, rr, rk2, p2 = _peer_retrieve(y, g, w_q, keys, tm=tm_retrieve)
    return _peer_dense(hbt, y, u, v, cnt, rr, rk2, p2, layer=layer, tm=tm, eb=eb, g_final=g_final)


PEER_TM = 512
PEER_DENSE_TM = 512
PEER_EB = 512
DENSE_TOKEN_SPLIT = 2
SEQ_ROWS = 256
S5_ROWS = 512
S5_LONG_ROWS = 256
CAST_ROWS = 1024


def _row(v):
    return v.reshape(1, -1)


@jax.jit
def _step(x_prompt, x_sample, state_s5_re, state_s5_im, state_lru_h, state_lru_conv,
          norm_mix, norm_ffn, norm_final,
          s5_lam_re, s5_lam_im, s5_log_dt, s5_b_re, s5_b_im, s5_c_re, s5_c_im, s5_d, s5_w_glu, s5_b_glu,
          lru_w_in, lru_conv_w, lru_conv_b, lru_w_a, lru_b_a, lru_w_i, lru_b_i, lru_lam, lru_w_out,
          peer_w_q, peer_sub_keys, peer_u, peer_v):
    bsz, seq, d = x_prompt.shape
    dec_b, dec_t, _ = x_sample.shape
    xp = x_prompt.reshape(bsz * seq, d)
    xs = x_sample.reshape(dec_b * dec_t, d)
    n_s = dec_b * dec_t

    pow_re, pow_im, bb_re, bb_im = _s5_discretize(
        s5_lam_re[0], s5_lam_im[0], s5_log_dt[0], s5_b_re[0], s5_b_im[0])
    g_mix0 = _row(norm_mix[0])
    w_glu = s5_w_glu[0].astype(BF16)
    wb, wc = _s5_proj_weights(bb_re, bb_im, s5_c_re[0], s5_c_im[0])

    def glu(x, ymix):
        return _s5_glu(x, g_mix0, ymix, _row(s5_d[0]), w_glu, _row(s5_b_glu[0]), tm=PEER_TM)

    ymix, sre, sim = _s5_scan_long(
        x_prompt, g_mix0, wb, wc,
        pow_re[0].reshape(S5_KB, SUBLANES, LANES), pow_im[0].reshape(S5_KB, SUBLANES, LANES),
        rows=S5_LONG_ROWS)
    yp = glu(xp, ymix.reshape(bsz * seq, d))

    def final(s):
        return jnp.transpose(s, (1, 0, 2, 3)).reshape(1, bsz, S5_GROUPS, S5_STATE)
    s5_p = (final(sre), final(sim))

    assert dec_t in (1, 2, 4) and n_s % S5_ROWS == 0

    def by_block(s):
        return jnp.transpose(s.reshape(dec_b, S5_KB, S5_SW), (1, 0, 2))
    ymix, sre, sim = _s5_scan_short(
        xs, g_mix0, wb, wc, _s5_short_consts(pow_re, dec_t), _s5_short_consts(pow_im, dec_t),
        (by_block(state_s5_re[0]), by_block(state_s5_im[0])), rows=S5_ROWS, seg=dec_t)
    ys = glu(xs, ymix)

    def by_batch(s):
        return jnp.transpose(s, (1, 0, 2)).reshape(1, dec_b, S5_GROUPS, S5_STATE)
    s5_s = (by_batch(sre), by_batch(sim))

    u_bf = _cast_bf16(peer_u, rows=CAST_ROWS)
    v_bf = _cast_bf16(peer_v, rows=CAST_ROWS)

    def peer_layer(i, y, g_final=None):
        return _peer(y, _row(norm_ffn[i]), peer_w_q[i].astype(BF16), peer_sub_keys[i].astype(BF16),
                     u_bf, v_bf, layer=i,
                     tm_retrieve=PEER_TM, tm=min(PEER_DENSE_TM, y.shape[0]), eb=PEER_EB, g_final=g_final)
    yp = peer_layer(0, yp)
    ys = peer_layer(0, ys)

    g_mix1 = _row(norm_mix[1])
    w_in = lru_w_in[0].astype(BF16)
    w_out = lru_w_out[0].astype(BF16)
    gate_args = (lru_conv_w[0], _row(lru_conv_b[0]), lru_w_a[0].astype(BF16), _row(lru_b_a[0]),
                 lru_w_i[0].astype(BF16), _row(lru_b_i[0]), _row(lru_lam[0]))

    zp = _matmul(yp, w_in, tm=PEER_TM, tn=D_MODEL, g=g_mix1, name="lru_in_proj")
    gated_p, hst_p, xbst_p = _lru(zp, *gate_args, n_seq=bsz, rows=SEQ_ROWS)
    yp = _matmul(gated_p, w_out, tm=PEER_TM, tn=D_MODEL, res=yp, name="lru_out_proj")
    lru_h_p = hst_p[:, 0, :].reshape(1, bsz, d)
    lru_c_p = xbst_p[:, SUBLANES - (CONV_WIDTH - 1):, :].reshape(1, bsz, CONV_WIDTH - 1, d)

    buf = state_lru_conv[0]
    zero = jnp.zeros((dec_b, 1, d), F32)
    pcv = jnp.stack([
        jnp.concatenate([buf[:, 2:3], zero, zero, zero], axis=1),
        jnp.concatenate([buf[:, 1:3], zero, zero], axis=1),
        jnp.concatenate([buf[:, 0:3], zero], axis=1),
    ]).reshape(CONV_WIDTH - 1, n_s, d)
    h0 = jnp.repeat(state_lru_h[0], dec_t, axis=0)
    zs = _matmul(ys, w_in, tm=PEER_TM, tn=D_MODEL, g=g_mix1, name="lru_in_proj")
    gated_s, hst_s, xbst_s = _lru(zs, *gate_args, n_seq=n_s // SEQ_ROWS, rows=SEQ_ROWS, pcv=pcv, h0=h0)
    ys = _matmul(gated_s, w_out, tm=PEER_TM, tn=D_MODEL, res=ys, name="lru_out_proj")
    lru_h_s = hst_s.reshape(dec_b, dec_t, d)[:, dec_t - 1].reshape(1, dec_b, d)
    lru_c_s = xbst_s.reshape(dec_b, dec_t, d)[:, 1:].reshape(1, dec_b, CONV_WIDTH - 1, d)

    g_fin = _row(norm_final)
    yp = peer_layer(1, yp, g_final=g_fin)
    ys = peer_layer(1, ys, g_final=g_fin)

    return (yp.reshape(bsz, seq, d), ys.reshape(dec_b, dec_t, d),
            s5_p[0], s5_p[1], s5_s[0], s5_s[1],
            lru_h_p, lru_c_p, lru_h_s, lru_c_s)


def kernel(x_prompt, x_sample, state_s5_re, state_s5_im, state_lru_h, state_lru_conv, norm_mix, norm_ffn, norm_final, s5_lam_re, s5_lam_im, s5_log_dt, s5_b_re, s5_b_im, s5_c_re, s5_c_im, s5_d, s5_w_glu, s5_b_glu, lru_w_in, lru_conv_w, lru_conv_b, lru_w_a, lru_b_a, lru_w_i, lru_b_i, lru_lam, lru_w_out, peer_w_q, peer_sub_keys, peer_u, peer_v):
    return _step(x_prompt, x_sample, state_s5_re, state_s5_im, state_lru_h, state_lru_conv,
                 norm_mix, norm_ffn, norm_final,
                 s5_lam_re, s5_lam_im, s5_log_dt, s5_b_re, s5_b_im, s5_c_re, s5_c_im, s5_d, s5_w_glu, s5_b_glu,
                 lru_w_in, lru_conv_w, lru_conv_b, lru_w_a, lru_b_a, lru_w_i, lru_b_i, lru_lam, lru_w_out,
                 peer_w_q, peer_sub_keys, peer_u, peer_v)
```

```python
import functools
import math

import jax
import jax.numpy as jnp
from jax import lax
from jax.experimental import pallas as pl
from jax.experimental.pallas import tpu as pltpu

F32 = jnp.float32
BF16 = jnp.bfloat16

D_MODEL = 2048
RMS_EPS = 1e-6
S5_GROUP = 16
S5_GROUPS = 128
S5_STATE = 64
S5_KB = 8
S5_KBW = 256
S5_SW = 1024
LRU_HEADS = 8
LRU_BLOCK = 256
CONV_WIDTH = 4
LRU_C = 8.0
PEER_HEADS = 8
N_KEYS = 128
N_EXPERTS = N_KEYS * N_KEYS
TOPK = 16
SUBLANES = 8
LANES = 128
BF16_ROWS = 2 * SUBLANES
VMEM_LIMIT_BYTES = 56 * 1024 * 1024

NEG_INF = float("-inf")


def _cparams(n_axes, flags=None):
    return pltpu.CompilerParams(
        dimension_semantics=("arbitrary",) * n_axes,
        vmem_limit_bytes=VMEM_LIMIT_BYTES,
        flags=flags,
    )


def _resident(block_shape, index_map):
    return pl.BlockSpec(block_shape, index_map, pipeline_mode=pl.Buffered(1))


def _rmsnorm(x, g):
    ms = jnp.mean(x * x, axis=-1, keepdims=True)
    return x * lax.rsqrt(ms + RMS_EPS) * g


def _gelu(x):
    return 0.5 * x * (1.0 + lax.erf(x * (1.0 / math.sqrt(2.0))))


def _dot(a, b):
    return jnp.dot(a, b, preferred_element_type=F32)


def _dot_nt(a, b):
    return lax.dot_general(a, b, (((1,), (1,)), ((), ())), preferred_element_type=F32)


def _dot_tn(a, b):
    return lax.dot_general(a, b, (((0,), (0,)), ((), ())), preferred_element_type=F32)


def _cast_kernel(x_ref, o_ref):
    o_ref[...] = x_ref[...].astype(BF16)


def _cast_bf16(x, *, rows):
    layers, n, d = x.shape
    spec = pl.BlockSpec((None, rows, d), lambda l, i: (l, i, 0))
    return pl.pallas_call(
        _cast_kernel, grid=(layers, n // rows), in_specs=[spec], out_specs=spec,
        out_shape=jax.ShapeDtypeStruct((layers, n, d), BF16),
        compiler_params=_cparams(2), name="cast_bf16",
    )(x)


def _s5_discretize_kernel(lre_ref, lim_ref, ldt_ref, bre_ref, bim_ref,
                          pre_ref, pim_ref, bbre_ref, bbim_ref):
    lr = lre_ref[...]
    li = lim_ref[...]
    dt = jnp.exp(ldt_ref[...])
    mag = jnp.exp(lr * dt)
    ab_re = mag * jnp.cos(li * dt)
    ab_im = mag * jnp.sin(li * dt)
    nr, ni = ab_re - 1.0, ab_im
    den = lr * lr + li * li
    f_re = (nr * lr + ni * li) / den
    f_im = (ni * lr - nr * li) / den
    for c in range(S5_GROUP):
        br = bre_ref[c]
        bi = bim_ref[c]
        bbre_ref[c] = f_re * br - f_im * bi
        bbim_ref[c] = f_re * bi + f_im * br
    p_re, p_im = ab_re, ab_im
    for k in range(SUBLANES):
        pre_ref[k] = p_re
        pim_ref[k] = p_im
        p_re, p_im = p_re * ab_re - p_im * ab_im, p_re * ab_im + p_im * ab_re


def _s5_discretize(lam_re, lam_im, log_dt, b_re, b_im):
    g, p = S5_GROUPS, S5_STATE
    b_re_t = jnp.transpose(b_re, (2, 0, 1))
    b_im_t = jnp.transpose(b_im, (2, 0, 1))
    out_shape = (
        jax.ShapeDtypeStruct((SUBLANES, g, p), F32),
        jax.ShapeDtypeStruct((SUBLANES, g, p), F32),
        jax.ShapeDtypeStruct((S5_GROUP, g, p), F32),
        jax.ShapeDtypeStruct((S5_GROUP, g, p), F32),
    )
    return pl.pallas_call(_s5_discretize_kernel, out_shape=out_shape, name="s5_discretize")(
        lam_re, lam_im, log_dt.reshape(g, 1), b_re_t, b_im_t)


def _s5_proj_weights(bb_re, bb_im, c_re, c_im):
    eye = jnp.eye(S5_GROUP, dtype=F32)

    def in_proj(bb):
        bb = bb.reshape(S5_GROUP, S5_KB, S5_GROUP, S5_STATE)
        w = jnp.einsum("ckgp,gh->kgchp", bb, eye)
        return w.reshape(S5_KB, S5_KBW, S5_SW)

    def out_proj(c):
        c = c.reshape(S5_KB, S5_GROUP, S5_GROUP, S5_STATE)
        w = jnp.einsum("kgcp,gh->kgphc", c, eye)
        return w.reshape(S5_KB, S5_SW, S5_KBW)

    wb = jnp.concatenate([in_proj(bb_re), in_proj(bb_im)], axis=2).astype(BF16)
    wc = jnp.concatenate([out_proj(c_re), out_proj(-c_im)], axis=1).astype(BF16)
    return wb, wc


def _s5_short_consts(pw, seg):
    pos = jnp.arange(SUBLANES) % seg
    pw = pw.reshape(SUBLANES, S5_KB, S5_SW)
    steps = [pw[d - 1][:, None, :] * (pos >= d).astype(F32)[None, :, None] for d in (1, 2)]
    init = jnp.transpose(pw[pos], (1, 0, 2))
    return jnp.stack(steps + [init], axis=1)


def _s5_scan_short_kernel(x_ref, g_ref, wb_ref, wc_ref, cre_ref, cim_ref, h0re_ref, h0im_ref,
                          ymix_ref, sre_ref, sim_ref, hb_scr, bu_scr, *, seg):
    kb = pl.program_id(1)
    rows = x_ref.shape[0]
    n_sq = SUBLANES // seg
    pos8 = lax.broadcasted_iota(jnp.int32, (SUBLANES, S5_SW), 0)

    def per_row(h_ref, sq):
        def seq_row(q):
            return jnp.broadcast_to(h_ref[0, pl.ds(sq + q, 1), :], (SUBLANES, S5_SW))
        out = seq_row(n_sq - 1)
        for q in range(n_sq - 2, -1, -1):
            out = jnp.where(pos8 < (q + 1) * seg, seq_row(q), out)
        return out

    @pl.when(kb == 0)
    def _():
        hb = _rmsnorm(x_ref[...], g_ref[...]).astype(BF16)
        for j in range(S5_KB):
            hb_scr[j] = hb[:, j * S5_KBW:(j + 1) * S5_KBW]

    bu_scr[...] = _dot(hb_scr[kb], wb_ref[0])

    def body(r, carry):
        row = pl.multiple_of(r * SUBLANES, SUBLANES)
        re = bu_scr[pl.ds(row, SUBLANES), 0:S5_SW]
        im = bu_scr[pl.ds(row, SUBLANES), S5_SW:2 * S5_SW]
        for idx, d in enumerate((1, 2)):
            ar = cre_ref[0, idx]
            ai = cim_ref[0, idx]
            sr = pltpu.roll(re, d, 0)
            si = pltpu.roll(im, d, 0)
            re, im = re + ar * sr - ai * si, im + ar * si + ai * sr
        pr = cre_ref[0, 2]
        pi = cim_ref[0, 2]
        sq = r * n_sq
        cr = per_row(h0re_ref, sq)
        ci = per_row(h0im_ref, sq)
        re, im = re + pr * cr - pi * ci, im + pr * ci + pi * cr
        bu_scr[pl.ds(row, SUBLANES), 0:S5_SW] = re
        bu_scr[pl.ds(row, SUBLANES), S5_SW:2 * S5_SW] = im
        for q in range(n_sq):
            last = (q + 1) * seg - 1
            sre_ref[0, pl.ds(sq + q, 1), :] = re[last:last + 1]
            sim_ref[0, pl.ds(sq + q, 1), :] = im[last:last + 1]
        return carry

    lax.fori_loop(0, rows // SUBLANES, body, 0)
    ymix_ref[...] = _dot(bu_scr[...].astype(BF16), wc_ref[0])


def _s5_scan_short(x, g, wb, wc, cre, cim, h0, *, rows, seg):
    n = x.shape[0]
    c_spec = pl.BlockSpec((1, 3, SUBLANES, S5_SW), lambda i, k: (k, 0, 0, 0))
    st_spec = pl.BlockSpec((1, rows // seg, S5_SW), lambda i, k: (k, i, 0))
    st_shape = jax.ShapeDtypeStruct((S5_KB, n // seg, S5_SW), F32)
    return pl.pallas_call(
        functools.partial(_s5_scan_short_kernel, seg=seg), grid=(n // rows, S5_KB),
        in_specs=[pl.BlockSpec((rows, D_MODEL), lambda i, k: (i, 0)),
                  pl.BlockSpec((1, D_MODEL), lambda i, k: (0, 0)),
                  pl.BlockSpec((1, S5_KBW, 2 * S5_SW), lambda i, k: (k, 0, 0)),
                  pl.BlockSpec((1, 2 * S5_SW, S5_KBW), lambda i, k: (k, 0, 0)),
                  c_spec, c_spec, st_spec, st_spec],
        out_specs=(pl.BlockSpec((rows, S5_KBW), lambda i, k: (i, k)), st_spec, st_spec),
        out_shape=(jax.ShapeDtypeStruct((n, D_MODEL), F32), st_shape, st_shape),
        scratch_shapes=[pltpu.VMEM((S5_KB, rows, S5_KBW), BF16), pltpu.VMEM((rows, 2 * S5_SW), F32)],
        compiler_params=_cparams(2), name="s5_scan_short",
    )(x, g, wb, wc, cre, cim, h0[0], h0[1])


S5_CT = 2 * S5_SW // LANES


def _s5_scan_long_kernel(x_ref, g_ref, wb_ref, wc_ref, are_ref, aim_ref,
                         ymix_ref, sre_ref, sim_ref, hb_scr, t_scr, carry_scr):
    tc = pl.program_id(0)
    kb = pl.program_id(1)
    n_seq, rows = x_ref.shape[0], x_ref.shape[1]
    n_grp = rows // SUBLANES
    half = S5_CT // 2

    @pl.when(kb == 0)
    def _():
        for b in range(n_seq):
            hb = _rmsnorm(x_ref[b], g_ref[...]).astype(BF16)
            for j in range(S5_KB):
                hb_scr[j, b * rows:(b + 1) * rows, :] = hb[:, j * S5_KBW:(j + 1) * S5_KBW]

    @pl.when(jnp.logical_and(tc == 0, kb == 0))
    def _():
        carry_scr[...] = jnp.zeros_like(carry_scr)

    bu = _dot(hb_scr[kb], wb_ref[0])
    for b in range(n_seq):
        for g in range(n_grp):
            r0 = b * rows + g * SUBLANES
            for ct in range(S5_CT):
                t0 = (g * S5_CT + ct) * SUBLANES
                t_scr[b, t0:t0 + SUBLANES, :] = bu[r0:r0 + SUBLANES, ct * LANES:(ct + 1) * LANES]

    ar = are_ref[0]
    ai = aim_ref[0]

    def body(g, carry):
        carry = list(carry)
        for s in range(SUBLANES):
            for b in range(n_seq):
                i_re = pl.ds(g * (S5_CT * SUBLANES) + s, SUBLANES, stride=SUBLANES)
                i_im = pl.ds(g * (S5_CT * SUBLANES) + half * SUBLANES + s, SUBLANES, stride=SUBLANES)
                sr, si = carry[2 * b], carry[2 * b + 1]
                sr, si = (ar * sr - ai * si + t_scr[b, i_re, :],
                          ar * si + ai * sr + t_scr[b, i_im, :])
                t_scr[b, i_re, :] = sr
                t_scr[b, i_im, :] = si
                carry[2 * b], carry[2 * b + 1] = sr, si
        return tuple(carry)

    c0 = tuple(carry_scr[kb, i] for i in range(2 * n_seq))
    c = lax.fori_loop(0, n_grp, body, c0)
    for b in range(n_seq):
        carry_scr[kb, 2 * b] = c[2 * b]
        carry_scr[kb, 2 * b + 1] = c[2 * b + 1]
        sre_ref[kb, b] = c[2 * b]
        sim_ref[kb, b] = c[2 * b + 1]

    row_blocks = []
    for b in range(n_seq):
        for g in range(n_grp):
            tiles = [t_scr[b, (g * S5_CT + ct) * SUBLANES:(g * S5_CT + ct + 1) * SUBLANES, :]
                     for ct in range(S5_CT)]
            row_blocks.append(jnp.concatenate(tiles, axis=1))
    states = jnp.concatenate(row_blocks, axis=0).astype(BF16)
    ymix = _dot(states, wc_ref[0])
    for b in range(n_seq):
        ymix_ref[b] = ymix[b * rows:(b + 1) * rows]


def _s5_scan_long(x, g, wb, wc, a_re, a_im, *, rows):
    n_seq, t_len, _ = x.shape
    grid = (t_len // rows, S5_KB)
    st_shape = jax.ShapeDtypeStruct((S5_KB, n_seq, SUBLANES, LANES), F32)
    st_spec = pl.BlockSpec((S5_KB, n_seq, SUBLANES, LANES), lambda t, k: (0, 0, 0, 0))
    a_spec = pl.BlockSpec((1, SUBLANES, LANES), lambda t, k: (k, 0, 0))
    return pl.pallas_call(
        _s5_scan_long_kernel, grid=grid,
        in_specs=[pl.BlockSpec((n_seq, rows, D_MODEL), lambda t, k: (0, t, 0)),
                  pl.BlockSpec((1, D_MODEL), lambda t, k: (0, 0)),
                  pl.BlockSpec((1, S5_KBW, 2 * S5_SW), lambda t, k: (k, 0, 0)),
                  pl.BlockSpec((1, 2 * S5_SW, S5_KBW), lambda t, k: (k, 0, 0)),
                  a_spec, a_spec],
        out_specs=(pl.BlockSpec((n_seq, rows, S5_KBW), lambda t, k: (0, t, k)), st_spec, st_spec),
        out_shape=(jax.ShapeDtypeStruct((n_seq, t_len, D_MODEL), F32), st_shape, st_shape),
        scratch_shapes=[pltpu.VMEM((S5_KB, n_seq * rows, S5_KBW), BF16),
                        pltpu.VMEM((n_seq, rows * S5_CT, LANES), F32),
                        pltpu.VMEM((S5_KB, 2 * n_seq, SUBLANES, LANES), F32)],
        compiler_params=_cparams(2), name="s5_scan_long",
    )(x, g, wb, wc, a_re, a_im)


def _s5_glu_kernel(x_ref, g_ref, ymix_ref, d_ref, w_ref, b_ref, o_ref):
    x = x_ref[...]
    h = _rmsnorm(x, g_ref[...])
    y = _gelu(ymix_ref[...] + d_ref[...] * h)
    z = _dot(y.astype(BF16), w_ref[...]) + b_ref[...]
    o_ref[...] = x + y * jax.nn.sigmoid(z)


def _s5_glu(x, g, ymix, d_skip, w_glu, b_glu, *, tm):
    n = x.shape[0]
    tile = pl.BlockSpec((tm, D_MODEL), lambda i: (i, 0))
    vec = pl.BlockSpec((1, D_MODEL), lambda i: (0, 0))
    return pl.pallas_call(
        _s5_glu_kernel, grid=(n // tm,),
        in_specs=[tile, vec, tile, vec, _resident((D_MODEL, D_MODEL), lambda i: (0, 0)), vec],
        out_specs=tile, out_shape=jax.ShapeDtypeStruct((n, D_MODEL), F32),
        compiler_params=_cparams(1), name="s5_glu",
    )(x, g, ymix, d_skip, w_glu, b_glu)


def _matmul_kernel(*refs, norm, residual):
    refs = list(refs)
    x_ref = refs.pop(0)
    g_ref = refs.pop(0) if norm else None
    w_ref = refs.pop(0)
    r_ref = refs.pop(0) if residual else None
    o_ref = refs.pop(0)
    x = x_ref[...]
    if norm:
        x = _rmsnorm(x, g_ref[...]).astype(BF16)
    acc = _dot(x, w_ref[...])
    if residual:
        acc = acc + r_ref[...]
    o_ref[...] = acc


def _matmul(x, w, *, tm, tn, g=None, res=None, name):
    n, k = x.shape
    n_out = w.shape[1]
    grid = (n_out // tn, n // tm)
    in_specs = [pl.BlockSpec((tm, k), lambda j, i: (i, 0))]
    args = [x]
    if g is not None:
        in_specs.append(pl.BlockSpec((1, k), lambda j, i: (0, 0)))
        args.append(g)
    in_specs.append(pl.BlockSpec((k, tn), lambda j, i: (0, j)))
    args.append(w)
    if res is not None:
        in_specs.append(pl.BlockSpec((tm, tn), lambda j, i: (i, j)))
        args.append(res)
    return pl.pallas_call(
        functools.partial(_matmul_kernel, norm=g is not None, residual=res is not None),
        grid=grid, in_specs=in_specs,
        out_specs=pl.BlockSpec((tm, tn), lambda j, i: (i, j)),
        out_shape=jax.ShapeDtypeStruct((n, n_out), F32),
        compiler_params=_cparams(2), name=name,
    )(*args)


def _lru_kernel(*refs, seg4):
    if seg4:
        (z_ref, cw_ref, cb_ref, wa_ref, ba_ref, wi_ref, bi_ref, lam_ref, pcv_ref, h0_ref,
         gated_ref, hst_ref, xbst_ref, a_scr, b_scr) = refs
    else:
        (z_ref, cw_ref, cb_ref, wa_ref, ba_ref, wi_ref, bi_ref, lam_ref,
         gated_ref, hst_ref, xbst_ref, a_scr, b_scr, prev_scr, carry_scr) = refs
    tc = pl.program_id(1)
    rows = z_ref.shape[0]
    d = D_MODEL
    gate = z_ref[:, 0:d]
    xb = z_ref[:, d:2 * d]

    row8 = lax.broadcasted_iota(jnp.int32, (SUBLANES, d), 0)
    if seg4:
        t_full = lax.broadcasted_iota(jnp.int32, (rows, d), 0) % CONV_WIDTH
        xc = cb_ref[...] + cw_ref[3:4, :] * xb
        for s in range(1, CONV_WIDTH):
            shifted = jnp.where(t_full >= s, pltpu.roll(xb, s, 0), pcv_ref[s - 1])
            xc = xc + cw_ref[3 - s:4 - s, :] * shifted
        xbst_ref[...] = xb
        rseg = row8 % CONV_WIDTH
    else:
        @pl.when(tc == 0)
        def _():
            prev_scr[...] = jnp.zeros_like(prev_scr)
            carry_scr[...] = jnp.zeros_like(carry_scr)
        xcat = jnp.concatenate([prev_scr[...], xb], axis=0)
        xc = cb_ref[...] + cw_ref[3:4, :] * xb
        for s in range(1, CONV_WIDTH):
            xc = xc + cw_ref[3 - s:4 - s, :] * pltpu.roll(xcat, s, 0)[SUBLANES:]
        tail = xb[rows - SUBLANES:]
        prev_scr[...] = tail
        xbst_ref[0] = tail
        rseg = row8

    r_parts, i_parts = [], []
    for hh in range(LRU_HEADS):
        xh = xc[:, hh * LRU_BLOCK:(hh + 1) * LRU_BLOCK].astype(BF16)
        r_parts.append(_dot(xh, wa_ref[hh]))
        i_parts.append(_dot(xh, wi_ref[hh]))
    r = jax.nn.sigmoid(jnp.concatenate(r_parts, axis=1) + ba_ref[...])
    ig = jax.nn.sigmoid(jnp.concatenate(i_parts, axis=1) + bi_ref[...])
    log_a = -LRU_C * r * jax.nn.softplus(-lam_ref[...])
    a = jnp.exp(log_a)
    a_scr[...] = a
    b_scr[...] = jnp.sqrt(-jnp.tanh(log_a) * (a * a + 1.0)) * (ig * xc)

    masks = [rseg >= s for s in (1, 2, 4)]

    def body(g, carry):
        row = pl.multiple_of(g * SUBLANES, SUBLANES)
        av = a_scr[pl.ds(row, SUBLANES), :]
        bv = b_scr[pl.ds(row, SUBLANES), :]
        for m, s in zip(masks, (1, 2, 4)):
            if seg4 and s == 4:
                continue
            a_sh = jnp.where(m, pltpu.roll(av, s, 0), 1.0)
            b_sh = jnp.where(m, pltpu.roll(bv, s, 0), 0.0)
            bv = bv + av * b_sh
            av = av * a_sh
        c = h0_ref[pl.ds(row, SUBLANES), :] if seg4 else carry
        hv = bv + av * c
        b_scr[pl.ds(row, SUBLANES), :] = hv
        if seg4:
            return carry
        return jnp.broadcast_to(hv[SUBLANES - 1:SUBLANES], (SUBLANES, d))

    if seg4:
        lax.fori_loop(0, rows // SUBLANES, body, 0)
        hst_ref[...] = b_scr[...]
    else:
        c = lax.fori_loop(0, rows // SUBLANES, body, carry_scr[...])
        carry_scr[...] = c
        hst_ref[0] = c
    gated_ref[...] = (b_scr[...] * _gelu(gate)).astype(BF16)


def _lru(z, conv_w, conv_b, w_a, b_a, w_i, b_i, lam, *, n_seq, rows, pcv=None, h0=None):
    n = z.shape[0]
    seg4 = h0 is not None
    d = D_MODEL
    n_chunk = n // (n_seq * rows)

    def row_map(s, t):
        return (s * n_chunk + t, 0)

    vec = pl.BlockSpec((1, d), lambda s, t: (0, 0))
    gate_w = pl.BlockSpec((LRU_HEADS, LRU_BLOCK, LRU_BLOCK), lambda s, t: (0, 0, 0))
    in_specs = [pl.BlockSpec((rows, 2 * d), row_map),
                pl.BlockSpec((CONV_WIDTH, d), lambda s, t: (0, 0)), vec,
                gate_w, vec, gate_w, vec, vec]
    args = [z, conv_w, conv_b, w_a, b_a, w_i, b_i, lam]
    scratch = [pltpu.VMEM((rows, d), F32), pltpu.VMEM((rows, d), F32)]
    tile = pl.BlockSpec((rows, d), row_map)
    if seg4:
        in_specs += [pl.BlockSpec((CONV_WIDTH - 1, rows, d), lambda s, t: (0, s * n_chunk + t, 0)), tile]
        args += [pcv, h0]
        st_shape = jax.ShapeDtypeStruct((n, d), F32)
        st_spec = tile
    else:
        scratch += [pltpu.VMEM((SUBLANES, d), F32), pltpu.VMEM((SUBLANES, d), F32)]
        st_shape = jax.ShapeDtypeStruct((n_seq, SUBLANES, d), F32)
        st_spec = pl.BlockSpec((1, SUBLANES, d), lambda s, t: (s, 0, 0))
    return pl.pallas_call(
        functools.partial(_lru_kernel, seg4=seg4),
        grid=(n_seq, n_chunk), in_specs=in_specs,
        out_specs=(tile, st_spec, st_spec),
        out_shape=(jax.ShapeDtypeStruct((n, d), BF16), st_shape, st_shape),
        scratch_shapes=scratch, compiler_params=_cparams(2),
        name="lru_seg4" if seg4 else "lru",
    )(*args)


NO_RANK = 31.0


def _top_rows(vals_scr, top_scr, n_rows):
    def body(r, carry):
        v = vals_scr[0:n_rows]
        m = jnp.max(v, axis=0, keepdims=True)
        top_scr[pl.ds(r, 1), :] = m
        vals_scr[0:n_rows] = jnp.where(v == m, NEG_INF, v)
        return carry
    lax.fori_loop(0, TOPK, body, 0)


def _top_rows_pair(va_scr, ta_scr, vb_scr, tb_scr, rank_scr):
    rank_scr[...] = jnp.full(rank_scr.shape, NO_RANK, F32)

    def body(r, carry):
        va = va_scr[...]
        vb = vb_scr[...]
        ma = jnp.max(va, axis=0, keepdims=True)
        mb = jnp.max(vb, axis=0, keepdims=True)
        ta_scr[pl.ds(r, 1), :] = ma
        tb_scr[pl.ds(r, 1), :] = mb
        hit_b = vb == mb
        rank_scr[...] = jnp.where(hit_b, lax.convert_element_type(r, F32), rank_scr[...])
        va_scr[...] = jnp.where(va == ma, NEG_INF, va)
        vb_scr[...] = jnp.where(hit_b, NEG_INF, vb)
        return carry
    lax.fori_loop(0, TOPK, body, 0)


def _peer_retrieve_kernel(y_ref, g_ref, wq_ref, keys_ref,
                          hbt_ref, cnt_ref, rr_ref, rk2_ref, p2_ref,
                          vals_scr, vals2_scr, ta_scr, tb_scr, cand_scr, top_scr, rank_scr):
    tm = y_ref.shape[0]
    h = _rmsnorm(y_ref[...], g_ref[...])
    hb = h.astype(BF16)
    hbt_ref[...] = jnp.transpose(hb)
    qb = _dot(hb, wq_ref[...]).astype(BF16)
    row8 = lax.broadcasted_iota(jnp.int32, (SUBLANES, tm), 0)
    for hh in range(PEER_HEADS):
        c0 = hh * 2 * N_KEYS
        s1 = _dot_nt(keys_ref[hh, 0], qb[:, c0:c0 + N_KEYS])
        s2 = _dot_nt(keys_ref[hh, 1], qb[:, c0 + N_KEYS:c0 + 2 * N_KEYS])
        vals_scr[...] = s1
        vals2_scr[...] = s2
        _top_rows_pair(vals_scr, ta_scr, vals2_scr, tb_scr, rank_scr)
        a = ta_scr[0:TOPK]
        b = tb_scr[0:TOPK]
        a0, b0 = a[0:1], b[0:1]
        a_lo, b_lo = a[0:SUBLANES], b[0:SUBLANES]
        cand_scr[0:16] = a0 + b
        cand_scr[16:24] = a[1:2] + b_lo
        cand_scr[24:32] = jnp.where(row8 < 5, a[2:3] + b_lo, NEG_INF)
        cand_scr[32:40] = jnp.where(row8 < 4, a[3:4] + b_lo, NEG_INF)
        cand_scr[40:48] = jnp.where(row8 < 3, a[4:5] + b_lo, NEG_INF)
        cand_scr[48:56] = a[SUBLANES:2 * SUBLANES] + b0
        cand_scr[56:64] = jnp.where(row8 >= 5, a_lo + b0, NEG_INF)
        cand_scr[64:72] = jnp.where(row8 >= 5, a_lo + b[1:2], NEG_INF)
        cand = cand_scr[...]
        vals_scr[0:72] = cand
        _top_rows(vals_scr, top_scr, 72)
        tau = top_scr[TOPK - 1:TOPK]
        z = jnp.sum(jnp.where(cand >= tau, jnp.exp(cand - (a0 + b0)), 0.0), axis=0, keepdims=True)
        cnt = jnp.zeros_like(s1)
        for r in range(TOPK):
            cnt = cnt + jnp.where(s1 + b[r:r + 1] >= tau, 1.0, 0.0)
        rr = 0.5 * jnp.exp(s1 - a0 - jnp.log(z))
        for p in range(N_KEYS // SUBLANES):
            cnt_ref[hh, p] = cnt[p * SUBLANES:(p + 1) * SUBLANES]
            rr_ref[hh, p] = rr[p * SUBLANES:(p + 1) * SUBLANES]
        rk2_ref[hh] = rank_scr[...].astype(BF16)
        p2_ref[hh] = jnp.exp(s2 - b0).astype(BF16)


def _peer_retrieve(y, g, w_q, keys, *, tm):
    n = y.shape[0]
    k1_shape = jax.ShapeDtypeStruct((PEER_HEADS, N_KEYS // SUBLANES, SUBLANES, n), F32)
    k2_shape = jax.ShapeDtypeStruct((PEER_HEADS, N_KEYS, n), BF16)
    k1_spec = pl.BlockSpec((PEER_HEADS, N_KEYS // SUBLANES, SUBLANES, tm), lambda i: (0, 0, 0, i))
    fac_spec = pl.BlockSpec((PEER_HEADS, N_KEYS, tm), lambda i: (0, 0, i))
    return pl.pallas_call(
        _peer_retrieve_kernel, grid=(n // tm,),
        in_specs=[pl.BlockSpec((tm, D_MODEL), lambda i: (i, 0)),
                  pl.BlockSpec((1, D_MODEL), lambda i: (0, 0)),
                  _resident((D_MODEL, D_MODEL), lambda i: (0, 0)),
                  pl.BlockSpec((PEER_HEADS, 2, N_KEYS, N_KEYS), lambda i: (0, 0, 0, 0))],
        out_specs=(pl.BlockSpec((D_MODEL, tm), lambda i: (0, i)),
                   k1_spec, k1_spec, fac_spec, fac_spec),
        out_shape=(jax.ShapeDtypeStruct((D_MODEL, n), BF16),
                   k1_shape, k1_shape, k2_shape, k2_shape),
        scratch_shapes=[pltpu.VMEM((N_KEYS, tm), F32), pltpu.VMEM((N_KEYS, tm), F32),
                        pltpu.VMEM((24, tm), F32),
                        pltpu.VMEM((24, tm), F32), pltpu.VMEM((72, tm), F32),
                        pltpu.VMEM((24, tm), F32), pltpu.VMEM((N_KEYS, tm), F32)],
        compiler_params=_cparams(1), name="peer_retrieve",
    )(y, g, w_q, keys)


def _peer_dense_kernel(*refs, eb, final_norm):
    if final_norm:
        (hbt_ref, y_ref, u_ref, vlo_ref, vhi_ref, cnt_ref, rr_ref, cntn_ref, rrn_ref, rk2_ref, p2_ref,
         gf_ref, o_ref, wa_scr, wb_scr, ga_scr, gb_scr, bc0_scr, bc1_scr) = refs
    else:
        (hbt_ref, y_ref, u_ref, vlo_ref, vhi_ref, cnt_ref, rr_ref, cntn_ref, rrn_ref, rk2_ref, p2_ref,
         o_ref, wa_scr, wb_scr, ga_scr, gb_scr, bc0_scr, bc1_scr) = refs
    bc_scrs = (bc0_scr, bc1_scr)
    j = pl.program_id(1)
    last = pl.num_programs(1) - 1
    n_sub = eb // N_KEYS
    tm = o_ref.shape[0]

    def gates(cnt_blk, rr_blk, half, dst_scr):
        bc_scr = bc_scrs[half]
        for hh in range(PEER_HEADS):
            for i in range(n_sub):
                r = half * n_sub + i
                k = (hh * n_sub + i) * BF16_ROWS
                bc_scr[0, k:k + BF16_ROWS, :] = jnp.broadcast_to(
                    cnt_blk[hh, 0, r:r + 1, :].astype(BF16), (BF16_ROWS, tm))
                bc_scr[1, k:k + BF16_ROWS, :] = jnp.broadcast_to(
                    rr_blk[hh, 0, r:r + 1, :].astype(BF16), (BF16_ROWS, tm))
        for i in range(n_sub):
            for c in range(N_KEYS // BF16_ROWS):
                rows = slice(c * BF16_ROWS, (c + 1) * BF16_ROWS)
                gate = None
                for hh in range(PEER_HEADS):
                    k = (hh * n_sub + i) * BF16_ROWS
                    cnt = bc_scr[0, k:k + BF16_ROWS, :]
                    rr = bc_scr[1, k:k + BF16_ROWS, :]
                    term = jnp.where(rk2_ref[hh, rows, :] < cnt, p2_ref[hh, rows, :] * rr,
                                     jnp.zeros((), BF16))
                    gate = term if gate is None else gate + term
                dst_scr[i * N_KEYS + c * BF16_ROWS:i * N_KEYS + (c + 1) * BF16_ROWS, :] = gate

    def up(half, g_scr, dst_scr):
        a = _dot(u_ref[half * eb:(half + 1) * eb, :], hbt_ref[...])
        gelu2 = a * (1.0 + lax.erf(a * (1.0 / math.sqrt(2.0))))
        dst_scr[...] = gelu2.astype(BF16) * g_scr[...]

    @pl.when(j == 0)
    def _():
        o_ref[...] = y_ref[...]
        wb_scr[...] = jnp.zeros_like(wb_scr)
        gates(cnt_ref, rr_ref, 0, ga_scr)

    @pl.when(j < last)
    def _():
        o_ref[...] += _dot_tn(wb_scr[...], vlo_ref[...])
        up(0, ga_scr, wa_scr)
        gates(cnt_ref, rr_ref, 1, gb_scr)
        o_ref[...] += _dot_tn(wa_scr[...], vhi_ref[...])

    @pl.when(j == last)
    def _():
        o_ref[...] += _dot_tn(wb_scr[...], vlo_ref[...])

    @pl.when(j < last)
    def _():
        up(1, gb_scr, wb_scr)
        gates(cntn_ref, rrn_ref, 0, ga_scr)

    if final_norm:
        @pl.when(j == last)
        def _():
            o_ref[...] = _rmsnorm(o_ref[...], gf_ref[...])


def _peer_dense(hbt, y, u, v, cnt, rr, rk2, p2, *, layer, tm, eb, g_final=None):
    n = y.shape[0]
    n_blk = N_EXPERTS // eb
    n_pair = n_blk // 2
    n_sub = eb // N_KEYS
    assert cnt.shape == (PEER_HEADS, n_pair, 2 * n_sub, n), (cnt.shape, eb)
    tile = pl.BlockSpec((tm, D_MODEL), lambda i, j: (i, 0))
    fac_blk = pl.BlockSpec((PEER_HEADS, 1, 2 * n_sub, tm),
                           lambda i, j: (0, jnp.minimum(j, n_pair - 1), 0, i))
    fac_next = pl.BlockSpec((PEER_HEADS, 1, 2 * n_sub, tm),
                            lambda i, j: (0, jnp.minimum(j + 1, n_pair - 1), 0, i))
    in_specs = [
        _resident((D_MODEL, tm), lambda i, j: (0, i)),
        _resident((tm, D_MODEL), lambda i, j: (i, 0)),
        pl.BlockSpec((None, 2 * eb, D_MODEL), lambda i, j: (layer, jnp.minimum(j, n_pair - 1), 0)),
        pl.BlockSpec((None, eb, D_MODEL), lambda i, j: (layer, jnp.maximum(2 * j - 1, 0), 0)),
        pl.BlockSpec((None, eb, D_MODEL), lambda i, j: (layer, jnp.minimum(2 * j, n_blk - 1), 0)),
        fac_blk, fac_blk, fac_next, fac_next,
        _resident((PEER_HEADS, N_KEYS, tm), lambda i, j: (0, 0, i)),
        _resident((PEER_HEADS, N_KEYS, tm), lambda i, j: (0, 0, i)),
    ]
    args = [hbt, y, u, v, v, cnt, rr, cnt, rr, rk2, p2]
    if g_final is not None:
        in_specs.append(pl.BlockSpec((1, D_MODEL), lambda i, j: (0, 0)))
        args.append(g_final)
    return pl.pallas_call(
        functools.partial(_peer_dense_kernel, eb=eb, final_norm=g_final is not None),
        grid=(n // tm, n_pair + 1), in_specs=in_specs, out_specs=tile,
        out_shape=jax.ShapeDtypeStruct((n, D_MODEL), F32),
        scratch_shapes=[pltpu.VMEM((eb, tm), BF16)] * 4
        + [pltpu.VMEM((2, PEER_HEADS * n_sub * BF16_ROWS, tm), BF16)] * 2,
        compiler_params=_cparams(2), name="peer_dense",
    )(*args)


def _peer(y, g, w_q, keys, u, v, *, layer, tm_retrieve, tm, eb, g_final=None):
    hbt, cnt, rr, rk2, p2 = _peer_retrieve(y, g, w_q, keys, tm=tm_retrieve)
    return _peer_dense(hbt, y, u, v, cnt, rr, rk2, p2, layer=layer, tm=tm, eb=eb, g_final=g_final)


PEER_TM = 512
PEER_DENSE_TM = 512
PEER_EB = 512
SEQ_ROWS = 256
S5_ROWS = 512
S5_LONG_ROWS = 256
CAST_ROWS = 1024


def _row(v):
    return v.reshape(1, -1)


@jax.jit
def _step(x_prompt, x_sample, state_s5_re, state_s5_im, state_lru_h, state_lru_conv,
          norm_mix, norm_ffn, norm_final,
          s5_lam_re, s5_lam_im, s5_log_dt, s5_b_re, s5_b_im, s5_c_re, s5_c_im, s5_d, s5_w_glu, s5_b_glu,
          lru_w_in, lru_conv_w, lru_conv_b, lru_w_a, lru_b_a, lru_w_i, lru_b_i, lru_lam, lru_w_out,
          peer_w_q, peer_sub_keys, peer_u, peer_v):
    bsz, seq, d = x_prompt.shape
    dec_b, dec_t, _ = x_sample.shape
    xp = x_prompt.reshape(bsz * seq, d)
    xs = x_sample.reshape(dec_b * dec_t, d)
    n_s = dec_b * dec_t

    pow_re, pow_im, bb_re, bb_im = _s5_discretize(
        s5_lam_re[0], s5_lam_im[0], s5_log_dt[0], s5_b_re[0], s5_b_im[0])
    g_mix0 = _row(norm_mix[0])
    w_glu = s5_w_glu[0].astype(BF16)
    wb, wc = _s5_proj_weights(bb_re, bb_im, s5_c_re[0], s5_c_im[0])

    def glu(x, ymix):
        return _s5_glu(x, g_mix0, ymix, _row(s5_d[0]), w_glu, _row(s5_b_glu[0]), tm=PEER_TM)

    ymix, sre, sim = _s5_scan_long(
        x_prompt, g_mix0, wb, wc,
        pow_re[0].reshape(S5_KB, SUBLANES, LANES), pow_im[0].reshape(S5_KB, SUBLANES, LANES),
        rows=S5_LONG_ROWS)
    yp = glu(xp, ymix.reshape(bsz * seq, d))

    def final(s):
        return jnp.transpose(s, (1, 0, 2, 3)).reshape(1, bsz, S5_GROUPS, S5_STATE)
    s5_p = (final(sre), final(sim))

    assert dec_t in (1, 2, 4) and n_s % S5_ROWS == 0

    def by_block(s):
        return jnp.transpose(s.reshape(dec_b, S5_KB, S5_SW), (1, 0, 2))
    ymix, sre, sim = _s5_scan_short(
        xs, g_mix0, wb, wc, _s5_short_consts(pow_re, dec_t), _s5_short_consts(pow_im, dec_t),
        (by_block(state_s5_re[0]), by_block(state_s5_im[0])), rows=S5_ROWS, seg=dec_t)
    ys = glu(xs, ymix)

    def by_batch(s):
        return jnp.transpose(s, (1, 0, 2)).reshape(1, dec_b, S5_GROUPS, S5_STATE)
    s5_s = (by_batch(sre), by_batch(sim))

    u_bf = _cast_bf16(peer_u, rows=CAST_ROWS)
    v_bf = _cast_bf16(peer_v, rows=CAST_ROWS)

    def peer_layer(i, y, g_final=None):
        return _peer(y, _row(norm_ffn[i]), peer_w_q[i].astype(BF16), peer_sub_keys[i].astype(BF16),
                     u_bf, v_bf, layer=i,
                     tm_retrieve=PEER_TM, tm=min(PEER_DENSE_TM, y.shape[0]), eb=PEER_EB, g_final=g_final)
    yp = peer_layer(0, yp)
    ys = peer_layer(0, ys)

    g_mix1 = _row(norm_mix[1])
    w_in = lru_w_in[0].astype(BF16)
    w_out = lru_w_out[0].astype(BF16)
    gate_args = (lru_conv_w[0], _row(lru_conv_b[0]), lru_w_a[0].astype(BF16), _row(lru_b_a[0]),
                 lru_w_i[0].astype(BF16), _row(lru_b_i[0]), _row(lru_lam[0]))

    zp = _matmul(yp, w_in, tm=PEER_TM, tn=D_MODEL, g=g_mix1, name="lru_in_proj")
    gated_p, hst_p, xbst_p = _lru(zp, *gate_args, n_seq=bsz, rows=SEQ_ROWS)
    yp = _matmul(gated_p, w_out, tm=PEER_TM, tn=D_MODEL, res=yp, name="lru_out_proj")
    lru_h_p = hst_p[:, 0, :].reshape(1, bsz, d)
    lru_c_p = xbst_p[:, SUBLANES - (CONV_WIDTH - 1):, :].reshape(1, bsz, CONV_WIDTH - 1, d)

    buf = state_lru_conv[0]
    zero = jnp.zeros((dec_b, 1, d), F32)
    pcv = jnp.stack([
        jnp.concatenate([buf[:, 2:3], zero, zero, zero], axis=1),
        jnp.concatenate([buf[:, 1:3], zero, zero], axis=1),
        jnp.concatenate([buf[:, 0:3], zero], axis=1),
    ]).reshape(CONV_WIDTH - 1, n_s, d)
    h0 = jnp.repeat(state_lru_h[0], dec_t, axis=0)
    zs = _matmul(ys, w_in, tm=PEER_TM, tn=D_MODEL, g=g_mix1, name="lru_in_proj")
    gated_s, hst_s, xbst_s = _lru(zs, *gate_args, n_seq=n_s // SEQ_ROWS, rows=SEQ_ROWS, pcv=pcv, h0=h0)
    ys = _matmul(gated_s, w_out, tm=PEER_TM, tn=D_MODEL, res=ys, name="lru_out_proj")
    lru_h_s = hst_s.reshape(dec_b, dec_t, d)[:, dec_t - 1].reshape(1, dec_b, d)
    lru_c_s = xbst_s.reshape(dec_b, dec_t, d)[:, 1:].reshape(1, dec_b, CONV_WIDTH - 1, d)

    g_fin = _row(norm_final)
    yp = peer_layer(1, yp, g_final=g_fin)
    ys = peer_layer(1, ys, g_final=g_fin)

    return (yp.reshape(bsz, seq, d), ys.reshape(dec_b, dec_t, d),
            s5_p[0], s5_p[1], s5_s[0], s5_s[1],
            lru_h_p, lru_c_p, lru_h_s, lru_c_s)


def kernel(x_prompt, x_sample, state_s5_re, state_s5_im, state_lru_h, state_lru_conv, norm_mix, norm_ffn, norm_final, s5_lam_re, s5_lam_im, s5_log_dt, s5_b_re, s5_b_im, s5_c_re, s5_c_im, s5_d, s5_w_glu, s5_b_glu, lru_w_in, lru_conv_w, lru_conv_b, lru_w_a, lru_b_a, lru_w_i, lru_b_i, lru_lam, lru_w_out, peer_w_q, peer_sub_keys, peer_u, peer_v):
    return _step(x_prompt, x_sample, state_s5_re, state_s5_im, state_lru_h, state_lru_conv,
                 norm_mix, norm_ffn, norm_final,
                 s5_lam_re, s5_lam_im, s5_log_dt, s5_b_re, s5_b_im, s5_c_re, s5_c_im, s5_d, s5_w_glu, s5_b_glu,
                 lru_w_in, lru_conv_w, lru_conv_b, lru_w_a, lru_b_a, lru_w_i, lru_b_i, lru_lam, lru_w_out,
                 peer_w_q, peer_sub_keys, peer_u, peer_v)
```
